```python
import math
import jax
import jax.numpy as jnp
from jax import lax
import numpy as np

D_MODEL = 2048
BATCH = 8
SEQ = 4096
DEPTH = 4

N_MIXERS = 2
N_RWKV = (DEPTH + 1) // 2
N_GDN = DEPTH // 2
ALPHA = (2 * DEPTH) ** 0.25
BETA = (8 * DEPTH) ** -0.25
LN_EPS = 1e-5
NORM_EPS = 1e-6

RW_HEAD = 64
RW_HEADS = D_MODEL // RW_HEAD
RW_LORA_DECAY = max(32, int(round(1.8 * D_MODEL ** 0.5 / 32)) * 32)
RW_LORA_AAA = max(32, int(round(1.8 * D_MODEL ** 0.5 / 32)) * 32)
RW_LORA_MV = max(32, int(round(1.3 * D_MODEL ** 0.5 / 32)) * 32)
RW_LORA_GATE = max(32, int(round(0.6 * D_MODEL ** 0.8 / 32)) * 32)
RW_LNX_EPS = 64e-5
MU_R = 0
MU_W = 1
MU_K = 2
MU_V = 3
MU_A = 4
MU_G = 5

GDN_HEAD_K = 128
GDN_HEAD_V = 128
GDN_K_HEADS = D_MODEL // GDN_HEAD_K
GDN_V_HEADS = 2 * GDN_K_HEADS
GDN_KEY_DIM = GDN_K_HEADS * GDN_HEAD_K
GDN_VALUE_DIM = GDN_V_HEADS * GDN_HEAD_V
GDN_CONV_CH = 2 * GDN_KEY_DIM + GDN_VALUE_DIM
GDN_PROJ = GDN_CONV_CH + GDN_VALUE_DIM + 2 * GDN_V_HEADS
GDN_CONV = 4
GDN_CHUNK = 64

D_FF = ((8 * D_MODEL // 3 + 127) // 128) * 128
FFN_CONV = 3

kernel_name = 'hybrid_rwkv7_gdn_convffn_deepnorm'


def layer_norm(x, g, b):
    xf = x.astype(jnp.float32)
    mu = jnp.mean(xf, -1, keepdims=True)
    var = jnp.mean(jnp.square(xf - mu), -1, keepdims=True)
    return ((xf - mu) * lax.rsqrt(var + LN_EPS) * g + b).astype(x.dtype)


def l2_normalize(x):
    xf = x.astype(jnp.float32)
    return (xf * lax.rsqrt(jnp.sum(xf * xf, -1, keepdims=True) + NORM_EPS)).astype(x.dtype)


def causal_dwconv(x, w):
    K = w.shape[0]
    T = x.shape[1]
    xp = jnp.pad(x, ((0, 0), (K - 1, 0), (0, 0)))
    out = w[K - 1] * x
    for k in range(K - 1):
        out = out + w[k] * xp[:, k:k + T]
    return out


def rwkv7_recurrence(r, w, k, v, a, b):
    B, T, H, N = r.shape
    decay = jnp.exp(-jnp.exp(w.astype(jnp.float32)))
    to_time = lambda t: jnp.moveaxis(t.astype(jnp.float32), 1, 0)

    def step(S, inp):
        r_t, d_t, k_t, v_t, a_t, b_t = inp
        sa = jnp.einsum('bhij,bhj->bhi', S, a_t)
        S = S * d_t[:, :, None, :] + sa[..., None] * b_t[:, :, None, :] + v_t[..., None] * k_t[:, :, None, :]
        return S, jnp.einsum('bhij,bhj->bhi', S, r_t)

    S0 = jnp.zeros((B, H, N, N), jnp.float32)
    _, y = lax.scan(step, S0, (to_time(r), jnp.moveaxis(decay, 1, 0), to_time(k), to_time(v), to_time(a), to_time(b)))
    return jnp.moveaxis(y, 0, 1).astype(r.dtype)


def rwkv7_time_mix(x, mu, w_r, w_k, w_v, w_o, w0, w1, w2, a0, a1, a2, g1, g2,
                   k_k, k_a, r_k, lnx_g, lnx_b, v_first, vres):
    B, T, D = x.shape
    H, N = RW_HEADS, RW_HEAD
    heads = lambda t: t.reshape(B, T, H, N)
    x_prev = jnp.pad(x, ((0, 0), (1, 0), (0, 0)))[:, :T]
    xx = x_prev - x
    xr = x + xx * mu[MU_R]
    xw = x + xx * mu[MU_W]
    xk = x + xx * mu[MU_K]
    xv = x + xx * mu[MU_V]
    xa = x + xx * mu[MU_A]
    xg = x + xx * mu[MU_G]

    r = xr @ w_r
    w = -jax.nn.softplus(-(w0 + jnp.tanh(xw @ w1) @ w2)) - 0.5
    k = xk @ w_k
    v = xv @ w_v
    if vres is None:
        v_first = v
    else:
        v0, v1, v2 = vres
        v = v + (v_first - v) * jax.nn.sigmoid(v0 + (xv @ v1) @ v2)
    a = jax.nn.sigmoid(a0 + (xa @ a1) @ a2)
    g = jax.nn.sigmoid(xg @ g1) @ g2

    kk = l2_normalize(heads(k * k_k))
    k = k * (1.0 + (a - 1.0) * k_a)
    y = rwkv7_recurrence(heads(r), heads(w), heads(k), heads(v), -kk, kk * heads(a))

    yf = y.astype(jnp.float32)
    mean = jnp.mean(yf, -1, keepdims=True)
    var = jnp.mean(jnp.square(yf - mean), -1, keepdims=True)
    y = (yf - mean) * lax.rsqrt(var + RW_LNX_EPS) * lnx_g.reshape(H, N) + lnx_b.reshape(H, N)
    y = y + jnp.sum(heads(r) * heads(k) * r_k, -1, keepdims=True) * heads(v)
    return (y.reshape(B, T, D).astype(x.dtype) * g) @ w_o, v_first


def chunk_gated_delta_rule(q, k, v, g, beta):
    B, T, H, DK = q.shape
    DV = v.shape[-1]
    C = GDN_CHUNK
    NC = T // C

    def chunks(t):
        t = jnp.moveaxis(t.astype(jnp.float32), 2, 1)
        return t.reshape((B, H, NC, C) + t.shape[3:])

    q, k, v, g, beta = chunks(q), chunks(k), chunks(v), chunks(g), chunks(beta)
    g = jnp.cumsum(g, axis=-1)
    idx = jnp.arange(C)
    causal = idx[:, None] >= idx[None, :]
    strict = idx[:, None] > idx[None, :]

    k_beta = k * beta[..., None]
    v_beta = v * beta[..., None]
    diff = g[..., :, None] - g[..., None, :]
    decay = jnp.exp(jnp.where(causal, diff, 0.0))
    A = jnp.where(strict, jnp.einsum('bhnik,bhnjk->bhnij', k_beta, k) * decay, 0.0)
    eye = jnp.broadcast_to(jnp.eye(C, dtype=jnp.float32), A.shape)
    Tm = lax.linalg.triangular_solve(A + eye, eye, left_side=True, lower=True, unit_diagonal=True)
    u = Tm @ v_beta
    wk = Tm @ (k_beta * jnp.exp(g)[..., None])

    def step(S, inp):
        q_n, k_n, u_n, w_n, g_n = inp
        d_n = g_n[..., :, None] - g_n[..., None, :]
        attn = jnp.where(causal, jnp.einsum('bhik,bhjk->bhij', q_n, k_n) * jnp.exp(jnp.where(causal, d_n, 0.0)), 0.0)
        v_new = u_n - jnp.einsum('bhck,bhkv->bhcv', w_n, S)
        o = (jnp.einsum('bhck,bhkv->bhcv', q_n * jnp.exp(g_n)[..., None], S)
             + jnp.einsum('bhij,bhjv->bhiv', attn, v_new))
        g_last = g_n[..., -1]
        S = (S * jnp.exp(g_last)[..., None, None]
             + jnp.einsum('bhck,bhcv->bhkv', k_n * jnp.exp(g_last[..., None] - g_n)[..., None], v_new))
        return S, o

    xs = tuple(jnp.moveaxis(t, 2, 0) for t in (q, k, u, wk, g))
    S0 = jnp.zeros((B, H, DK, DV), jnp.float32)
    _, o = lax.scan(step, S0, xs)
    o = jnp.moveaxis(o, 0, 2).reshape(B, H, T, DV)
    return jnp.moveaxis(o, 1, 2)


def gated_deltanet(x, w_in, conv_w, a_log, dt_bias, norm_w, w_out):
    B, T, D = x.shape
    proj = x @ w_in
    qkv, z, b, a = jnp.split(proj, [GDN_CONV_CH, GDN_CONV_CH + GDN_VALUE_DIM,
                                    GDN_CONV_CH + GDN_VALUE_DIM + GDN_V_HEADS], axis=-1)
    qkv = jax.nn.silu(causal_dwconv(qkv, conv_w))
    q, k, v = jnp.split(qkv, [GDN_KEY_DIM, 2 * GDN_KEY_DIM], axis=-1)
    rep = GDN_V_HEADS // GDN_K_HEADS
    q = jnp.repeat(l2_normalize(q.reshape(B, T, GDN_K_HEADS, GDN_HEAD_K)), rep, axis=2) * (GDN_HEAD_K ** -0.5)
    k = jnp.repeat(l2_normalize(k.reshape(B, T, GDN_K_HEADS, GDN_HEAD_K)), rep, axis=2)
    v = v.reshape(B, T, GDN_V_HEADS, GDN_HEAD_V)
    beta = jax.nn.sigmoid(b.astype(jnp.float32))
    g = -jnp.exp(a_log.astype(jnp.float32)) * jax.nn.softplus((a + dt_bias).astype(jnp.float32))
    o = chunk_gated_delta_rule(q, k, v, g, beta)
    o = o * lax.rsqrt(jnp.mean(o * o, -1, keepdims=True) + NORM_EPS) * norm_w
    o = o * jax.nn.silu(z.reshape(B, T, GDN_V_HEADS, GDN_HEAD_V).astype(jnp.float32))
    return o.reshape(B, T, GDN_VALUE_DIM).astype(x.dtype) @ w_out


def conv_ffn(x, w_up, conv_w, conv_b, w_down):
    h = causal_dwconv(x @ w_up, conv_w) + conv_b
    gate, up = jnp.split(h, 2, axis=-1)
    return (jax.nn.silu(gate) * up) @ w_down


def setup_inputs(seed: int = 0) -> dict:
    key = jax.random.key(seed)
    keys = iter(jax.random.split(key, 40))

    def nrm(shape, scale):
        return scale * jax.random.normal(next(keys), shape, jnp.float32)

    def uni(shape, lo, hi):
        return jax.random.uniform(next(keys), shape, jnp.float32, lo, hi)

    D, R, G, L = D_MODEL, N_RWKV, N_GDN, DEPTH
    x = nrm((BATCH, SEQ, D), 1.0)
    rw_mu = uni((R, 6, D), 0.0, 1.0)
    rw_w_r = nrm((R, D, D), D ** -0.5)
    rw_w_k = nrm((R, D, D), D ** -0.5)
    rw_w_v = nrm((R, D, D), D ** -0.5)
    rw_w_o = nrm((R, D, D), BETA * D ** -0.5)
    rw_w0 = uni((R, D), -6.0, 0.0)
    rw_w1 = nrm((R, D, RW_LORA_DECAY), D ** -0.5)
    rw_w2 = nrm((R, RW_LORA_DECAY, D), 0.5 * RW_LORA_DECAY ** -0.5)
    rw_a0 = nrm((R, D), 0.5)
    rw_a1 = nrm((R, D, RW_LORA_AAA), D ** -0.5)
    rw_a2 = nrm((R, RW_LORA_AAA, D), RW_LORA_AAA ** -0.5)
    rw_v0 = nrm((R - 1, D), 0.5)
    rw_v1 = nrm((R - 1, D, RW_LORA_MV), D ** -0.5)
    rw_v2 = nrm((R - 1, RW_LORA_MV, D), RW_LORA_MV ** -0.5)
    rw_g1 = nrm((R, D, RW_LORA_GATE), D ** -0.5)
    rw_g2 = nrm((R, RW_LORA_GATE, D), RW_LORA_GATE ** -0.5)
    rw_k_k = 0.85 + nrm((R, D), 0.05)
    rw_k_a = 1.0 + nrm((R, D), 0.05)
    rw_r_k = nrm((R, RW_HEADS, RW_HEAD), 0.1)
    rw_lnx_g = 1.0 + nrm((R, D), 0.05)
    rw_lnx_b = nrm((R, D), 0.02)
    gdn_w_in = nrm((G, D, GDN_PROJ), D ** -0.5)
    gdn_conv_w = nrm((G, GDN_CONV, GDN_CONV_CH), GDN_CONV ** -0.5)
    gdn_a_log = jnp.log(uni((G, GDN_V_HEADS), 1.0, 16.0))
    dt = jnp.exp(uni((G, GDN_V_HEADS), math.log(1e-3), math.log(1e-1)))
    gdn_dt_bias = dt + jnp.log(-jnp.expm1(-dt))
    gdn_norm_w = 1.0 + nrm((G, GDN_HEAD_V), 0.05)
    gdn_w_out = nrm((G, GDN_VALUE_DIM, D), BETA * GDN_VALUE_DIM ** -0.5)
    ffn_w_up = nrm((L, D, 2 * D_FF), D ** -0.5)
    ffn_conv_w = nrm((L, FFN_CONV, 2 * D_FF), FFN_CONV ** -0.5)
    ffn_conv_b = nrm((L, 2 * D_FF), 0.02)
    ffn_w_down = nrm((L, D_FF, D), BETA * D_FF ** -0.5)
    ln_mix_g = 1.0 + nrm((L, D), 0.05)
    ln_mix_b = nrm((L, D), 0.02)
    ln_ffn_g = 1.0 + nrm((L, D), 0.05)
    ln_ffn_b = nrm((L, D), 0.02)
    return {'x': x, 'rw_mu': rw_mu, 'rw_w_r': rw_w_r, 'rw_w_k': rw_w_k, 'rw_w_v': rw_w_v,
            'rw_w_o': rw_w_o, 'rw_w0': rw_w0, 'rw_w1': rw_w1, 'rw_w2': rw_w2, 'rw_a0': rw_a0,
            'rw_a1': rw_a1, 'rw_a2': rw_a2, 'rw_v0': rw_v0, 'rw_v1': rw_v1, 'rw_v2': rw_v2,
            'rw_g1': rw_g1, 'rw_g2': rw_g2, 'rw_k_k': rw_k_k, 'rw_k_a': rw_k_a, 'rw_r_k': rw_r_k,
            'rw_lnx_g': rw_lnx_g, 'rw_lnx_b': rw_lnx_b, 'gdn_w_in': gdn_w_in,
            'gdn_conv_w': gdn_conv_w, 'gdn_a_log': gdn_a_log, 'gdn_dt_bias': gdn_dt_bias,
            'gdn_norm_w': gdn_norm_w, 'gdn_w_out': gdn_w_out, 'ffn_w_up': ffn_w_up,
            'ffn_conv_w': ffn_conv_w, 'ffn_conv_b': ffn_conv_b, 'ffn_w_down': ffn_w_down,
            'ln_mix_g': ln_mix_g, 'ln_mix_b': ln_mix_b, 'ln_ffn_g': ln_ffn_g, 'ln_ffn_b': ln_ffn_b}


def reference(x, rw_mu, rw_w_r, rw_w_k, rw_w_v, rw_w_o, rw_w0, rw_w1, rw_w2, rw_a0,
              rw_a1, rw_a2, rw_v0, rw_v1, rw_v2, rw_g1, rw_g2, rw_k_k, rw_k_a, rw_r_k,
              rw_lnx_g, rw_lnx_b, gdn_w_in, gdn_conv_w, gdn_a_log, gdn_dt_bias,
              gdn_norm_w, gdn_w_out, ffn_w_up, ffn_conv_w, ffn_conv_b, ffn_w_down,
              ln_mix_g, ln_mix_b, ln_ffn_g, ln_ffn_b):
    v_first = None
    for i in range(DEPTH):
        j = i // N_MIXERS
        if i % N_MIXERS == 0:
            vres = None if j == 0 else (rw_v0[j - 1], rw_v1[j - 1], rw_v2[j - 1])
            mix, v_first = rwkv7_time_mix(
                x, rw_mu[j], rw_w_r[j], rw_w_k[j], rw_w_v[j], rw_w_o[j], rw_w0[j], rw_w1[j],
                rw_w2[j], rw_a0[j], rw_a1[j], rw_a2[j], rw_g1[j], rw_g2[j], rw_k_k[j],
                rw_k_a[j], rw_r_k[j], rw_lnx_g[j], rw_lnx_b[j], v_first, vres)
        else:
            mix = gated_deltanet(x, gdn_w_in[j], gdn_conv_w[j], gdn_a_log[j],
                                 gdn_dt_bias[j], gdn_norm_w[j], gdn_w_out[j])
        x = layer_norm(ALPHA * x + mix, ln_mix_g[i], ln_mix_b[i])
        x = layer_norm(ALPHA * x + conv_ffn(x, ffn_w_up[i], ffn_conv_w[i], ffn_conv_b[i], ffn_w_down[i]),
                       ln_ffn_g[i], ln_ffn_b[i])
    return x
```

```python
import functools
import math

import jax
import jax.numpy as jnp
from jax import lax
from jax.experimental import pallas as pl
from jax.experimental.pallas import tpu as pltpu

F32 = jnp.float32
BF16 = jnp.bfloat16

RW_HEAD = 64
GDN_HEAD = 128
LN_EPS = 1e-5
NORM_EPS = 1e-6
RW_LNX_EPS = 64e-5

V7X_LANES = 128
V7X_SUBLANES = 8
V7X_MXU_DIM = 256
V7X_VMEM_BYTES = 64 * 1024 * 1024
VMEM_LIMIT = V7X_VMEM_BYTES * 7 // 8

CHUNK = 64
GROUP = V7X_MXU_DIM
HALO = V7X_SUBLANES


def _params(*sem):
    return pltpu.CompilerParams(dimension_semantics=sem, vmem_limit_bytes=VMEM_LIMIT)


def _pick(n, prefs):
    for p in prefs:
        if p <= n and n % p == 0:
            return p
    return n


def _round_up(n, m):
    return (n + m - 1) // m * m


def _dot(a, b):
    return jnp.dot(a.astype(BF16), b.astype(BF16), preferred_element_type=F32)


def _dot_nt(a, b):
    return lax.dot_general(a.astype(BF16), b.astype(BF16), (((1,), (1,)), ((), ())),
                           preferred_element_type=F32)


def _dot_tn(a, b):
    return lax.dot_general(a.astype(BF16), b.astype(BF16), (((0,), (0,)), ((), ())),
                           preferred_element_type=F32)


def _split(x, terms):
    out = []
    rest = x
    for _ in range(terms):
        t = rest.astype(BF16)
        out.append(t)
        rest = rest - t.astype(F32)
    return out


def _dot_sel_left(sel, x, terms=3):
    acc = None
    for t in _split(x, terms):
        p = jnp.dot(sel, t, preferred_element_type=F32)
        acc = p if acc is None else acc + p
    return acc


def _dot_sel_right(x, sel, terms=3):
    acc = None
    for t in _split(x, terms):
        p = jnp.dot(t, sel, preferred_element_type=F32)
        acc = p if acc is None else acc + p
    return acc


def _iota(shape, dim):
    return lax.broadcasted_iota(jnp.int32, shape, dim)


def _seg(shape, dim, seg):
    return lax.shift_right_logical(_iota(shape, dim), int(math.log2(seg)))


def _pos(shape, dim, seg):
    return lax.bitwise_and(_iota(shape, dim), seg - 1)


def _block_ones(n, seg):
    return jnp.where(_seg((n, n), 0, seg) == _seg((n, n), 1, seg), 1.0, 0.0).astype(BF16)


def _block_tri(n, seg, upper):
    r, c = _iota((n, n), 0), _iota((n, n), 1)
    same = _seg((n, n), 0, seg) == _seg((n, n), 1, seg)
    tri = (r <= c) if upper else (r >= c)
    return jnp.where(jnp.logical_and(same, tri), 1.0, 0.0).astype(BF16)


def _segsum(x, ones_blk):
    return _dot_sel_right(x, ones_blk, terms=2)


def _block_diag(x, row_seg, lane_seg):
    rows, lanes = x.shape
    n = lanes // lane_seg
    tall = jnp.concatenate([x] * n, axis=0)
    shape = (n * rows, lanes)
    return jnp.where(_seg(shape, 0, row_seg) == _seg(shape, 1, lane_seg), tall, 0.0)


def _neumann_inverse(nmat, eye_l, heads_seg):
    c = nmat.shape[0]
    bd = lambda m: _block_diag(m, c, c)
    p = eye_l + nmat
    nk = _dot(nmat, bd(nmat))
    n = 2
    while 2 * n < c:
        both = _dot(jnp.concatenate([p, nk], axis=0), bd(nk))
        p = p + both[:c]
        nk = both[c:]
        n *= 2
    return p + _dot(p, bd(nk))


def _layer_norm(y, g, b):
    mu = jnp.mean(y, axis=-1, keepdims=True)
    yc = y - mu
    var = jnp.mean(yc * yc, axis=-1, keepdims=True)
    return yc * lax.rsqrt(var + LN_EPS) * g + b


def _mm_kernel(x_ref, w_ref, o_ref):
    o_ref[...] = jnp.dot(x_ref[...], w_ref[...], preferred_element_type=F32).astype(o_ref.dtype)


def _matmul_stack(xs, ws, out_dtype):
    p, m, k = xs.shape
    n = ws.shape[2]
    bm = _pick(m, (1024, 512, 256, 128))
    bn = _pick(n, (1024, 512, 256, 128))
    return pl.pallas_call(
        _mm_kernel,
        grid=(p, n // bn, m // bm),
        in_specs=[pl.BlockSpec((None, bm, k), lambda q, j, i: (q, i, 0)),
                  pl.BlockSpec((None, k, bn), lambda q, j, i: (q, 0, j))],
        out_specs=pl.BlockSpec((None, bm, bn), lambda q, j, i: (q, i, j)),
        out_shape=jax.ShapeDtypeStruct((p, m, n), out_dtype),
        compiler_params=_params("parallel", "parallel", "parallel"),
        name="matmul_stack",
    )(xs, ws)


def _mm_res_ln_kernel(a_ref, w_ref, res_ref, gb_ref, o32_ref, o16_ref, acc_ref, *, alpha, nk):
    kk = pl.program_id(1)
    part = jnp.dot(a_ref[...], w_ref[...], preferred_element_type=F32)

    @pl.when(kk == 0)
    def _():
        acc_ref[...] = part

    @pl.when(kk > 0)
    def _():
        acc_ref[...] += part

    @pl.when(kk == nk - 1)
    def _():
        y = alpha * res_ref[...] + acc_ref[...]
        out = _layer_norm(y, gb_ref[0:1, :], gb_ref[1:2, :])
        o32_ref[...] = out
        o16_ref[...] = out.astype(BF16)


def _matmul_res_ln(a, w, resid, gamma, beta, alpha):
    m, k = a.shape
    d = w.shape[1]
    bm = _pick(m, (512, 256, 128))
    bk = _pick(k, (1408, 1024, 512, 256, 128))
    nk = k // bk
    gb = jnp.concatenate([gamma.reshape(1, d), beta.reshape(1, d),
                          jnp.zeros((V7X_SUBLANES - 2, d), F32)], axis=0)
    return pl.pallas_call(
        functools.partial(_mm_res_ln_kernel, alpha=alpha, nk=nk),
        grid=(m // bm, nk),
        in_specs=[pl.BlockSpec((bm, bk), lambda i, kk: (i, kk)),
                  pl.BlockSpec((bk, d), lambda i, kk: (kk, 0)),
                  pl.BlockSpec((bm, d), lambda i, kk: (i, 0)),
                  pl.BlockSpec((V7X_SUBLANES, d), lambda i, kk: (0, 0))],
        out_specs=[pl.BlockSpec((bm, d), lambda i, kk: (i, 0)),
                   pl.BlockSpec((bm, d), lambda i, kk: (i, 0))],
        out_shape=[jax.ShapeDtypeStruct((m, d), F32), jax.ShapeDtypeStruct((m, d), BF16)],
        scratch_shapes=[pltpu.VMEM((bm, d), F32)],
        compiler_params=_params("parallel", "arbitrary"),
        name="matmul_res_ln",
    )(a, w, resid, gb)


def _causal_conv(h, hh, cw, taps, buf_ref):
    bm = h.shape[0]
    buf_ref[0:HALO, :] = hh
    buf_ref[HALO:HALO + bm, :] = h
    out = cw[taps - 1:taps, :] * h
    for k in range(taps - 1):
        shift = taps - 1 - k
        out = out + cw[k:k + 1, :] * buf_ref[HALO - shift:HALO - shift + bm, :]
    return out


def _mm_conv_kernel(x_ref, xh_ref, *refs, taps, glu, seq_len, bm):
    if glu:
        wg_ref, wu_ref, cg_ref, cu_ref, o_ref, buf_ref = refs
    else:
        wg_ref, cg_ref, o_ref, buf_ref = refs
    i = pl.program_id(1)
    valid = jnp.where((i * bm) % seq_len == 0, 0.0, 1.0).astype(F32)
    x = x_ref[...]
    xh = xh_ref[...]

    def branch(w_ref, c_ref):
        h = jnp.dot(x, w_ref[...], preferred_element_type=F32)
        hh = jnp.dot(xh, w_ref[...], preferred_element_type=F32) * valid
        c = c_ref[...]
        return _causal_conv(h, hh, c, taps, buf_ref) + c[taps:taps + 1, :]

    gate = branch(wg_ref, cg_ref)
    act = gate * jax.nn.sigmoid(gate)
    if glu:
        act = act * branch(wu_ref, cu_ref)
    o_ref[...] = act.astype(o_ref.dtype)


def _conv_table(conv_w, bias, n_pad):
    taps, n = conv_w.shape
    rows = [conv_w, (jnp.zeros((1, n), F32) if bias is None else bias.reshape(1, n)),
            jnp.zeros((V7X_SUBLANES - taps - 1, n), F32)]
    tab = jnp.concatenate(rows, axis=0)
    return jnp.pad(tab, ((0, 0), (0, n_pad - n)))


def _matmul_conv(x16, w, tab, seq_len, taps, out_dtype, w_up=None, tab_up=None):
    m, k = x16.shape
    n = w.shape[1]
    glu = w_up is not None
    bm = _pick(seq_len, (1024, 512, 256, 128, 64, 32, 16, 8))
    bn = _pick(n, (512, 256, 128))
    hb = bm // HALO
    x_spec = pl.BlockSpec((bm, k), lambda j, i: (i, 0))
    xh_spec = pl.BlockSpec((HALO, k), lambda j, i: (jnp.maximum(i * hb - 1, 0), 0))
    w_spec = pl.BlockSpec((k, bn), lambda j, i: (0, j))
    c_spec = pl.BlockSpec((V7X_SUBLANES, bn), lambda j, i: (0, j))
    if glu:
        in_specs = [x_spec, xh_spec, w_spec, w_spec, c_spec, c_spec]
        args = (x16, x16, w, w_up, tab, tab_up)
    else:
        in_specs = [x_spec, xh_spec, w_spec, c_spec]
        args = (x16, x16, w, tab)
    return pl.pallas_call(
        functools.partial(_mm_conv_kernel, taps=taps, glu=glu, seq_len=seq_len, bm=bm),
        grid=(n // bn, m // bm),
        in_specs=in_specs,
        out_specs=pl.BlockSpec((bm, bn), lambda j, i: (i, j)),
        out_shape=jax.ShapeDtypeStruct((m, n), out_dtype),
        scratch_shapes=[pltpu.VMEM((bm + HALO, bn), F32)],
        compiler_params=_params("parallel", "parallel"),
        name="matmul_conv_glu" if glu else "matmul_conv",
    )(*args)


def _rwkv_in_kernel(x_ref, xh_ref, mu_ref, wd_ref, xm_ref, lh_ref, *, seq_len, bm, has_vres):
    i = pl.program_id(0)
    valid = jnp.where((i * bm) % seq_len == 0, 0.0, 1.0).astype(F32)
    x = x_ref[...]
    prev_row = xh_ref[HALO - 1:HALO, :] * valid
    rolled = pltpu.roll(x, shift=1, axis=0)
    x_prev = jnp.where(_iota(x.shape, 0) == 0, prev_row, rolled)
    xx = x_prev - x
    mu = mu_ref[...]
    mix = lambda r: (x + xx * mu[r:r + 1, :]).astype(BF16)
    xm_ref[0] = mix(0)
    xm_ref[1] = mix(2)
    xv = mix(3)
    xm_ref[2] = xv
    lp = V7X_LANES
    down = lambda xin, lo, hi: jnp.dot(xin, wd_ref[:, lo:hi], preferred_element_type=F32)
    lh_ref[:, 0:lp] = jnp.tanh(down(mix(1), 0, lp)).astype(BF16)
    lh_ref[:, lp:2 * lp] = down(mix(4), lp, 2 * lp).astype(BF16)
    if has_vres:
        lh_ref[:, 2 * lp:3 * lp] = down(xv, 2 * lp, 3 * lp).astype(BF16)
    else:
        lh_ref[:, 2 * lp:3 * lp] = jnp.zeros((bm, lp), BF16)
    lh_ref[:, 3 * lp:] = jax.nn.sigmoid(down(mix(5), 3 * lp, wd_ref.shape[1])).astype(BF16)


def _rwkv_in(x, mu8, wd, seq_len, has_vres):
    m, d = x.shape
    lh = wd.shape[1]
    bm = _pick(seq_len, (512, 256, 128, 64, 32, 16, 8))
    hb = bm // HALO
    return pl.pallas_call(
        functools.partial(_rwkv_in_kernel, seq_len=seq_len, bm=bm, has_vres=has_vres),
        grid=(m // bm,),
        in_specs=[pl.BlockSpec((bm, d), lambda i: (i, 0)),
                  pl.BlockSpec((HALO, d), lambda i: (jnp.maximum(i * hb - 1, 0), 0)),
                  pl.BlockSpec((V7X_SUBLANES, d), lambda i: (0, 0)),
                  pl.BlockSpec((d, lh), lambda i: (0, 0))],
        out_specs=[pl.BlockSpec((3, bm, d), lambda i: (0, i, 0)),
                   pl.BlockSpec((bm, lh), lambda i: (i, 0))],
        out_shape=[jax.ShapeDtypeStruct((3, m, d), BF16), jax.ShapeDtypeStruct((m, lh), BF16)],
        compiler_params=_params("parallel"),
        name="rwkv_in",
    )(x, x, mu8, wd)


def _rwkv_chunk_kernel(r_ref, k_ref, v_ref, vf_ref, lh_ref, wup_ref, g2_ref, vec_ref, z_ref,
                       ht_ref, r_s, k_s, v_s, a_s, b_s, lw_s, y_s, *, has_vres, n_chunks):
    c = CHUNK
    hd = RW_HEAD
    lp = V7X_LANES

    @pl.when(pl.program_id(2) == 0)
    def _():
        ht_ref[...] = jnp.zeros(ht_ref.shape, F32)

    ones_blk = _block_ones(GROUP, hd)
    segsum = lambda t: _segsum(t, ones_blk)

    vec = vec_ref[...]
    w0, a0, v0, k_k, k_a, r_k, lnx_g, lnx_b = [vec[i:i + 1, :] for i in range(8)]
    r = r_ref[...]
    k = k_ref[...]
    v = v_ref[...]
    up = lambda lo, hi, w: jnp.dot(lh_ref[:, lo:hi], w, preferred_element_type=F32)
    w = -jax.nn.softplus(-(w0 + up(0, lp, wup_ref[0]))) - 0.5
    a = jax.nn.sigmoid(a0 + up(lp, 2 * lp, wup_ref[1]))
    if has_vres:
        v = v + (vf_ref[...] - v) * jax.nn.sigmoid(v0 + up(2 * lp, 3 * lp, wup_ref[2]))
    gate = up(3 * lp, lh_ref.shape[1], g2_ref[...])
    kk = k * k_k
    kk = kk * lax.rsqrt(segsum(kk * kk) + NORM_EPS)
    k2 = k * (1.0 + (a - 1.0) * k_a)
    r_s[...] = r
    k_s[...] = k2
    v_s[...] = v
    a_s[...] = -kk
    b_s[...] = kk * a
    lw_s[...] = -jnp.exp(w)

    row = _iota((c, GROUP), 0)
    col = _pos((c, GROUP), 1, c)
    strict = row > col
    incl = row >= col
    eye_l = jnp.where(row == col, 1.0, 0.0).astype(F32)
    tri = _block_tri(c, c, upper=False)
    bd = lambda t: _block_diag(t, c, hd)
    bd_mask = _seg((GROUP, GROUP), 0, hd) == _seg((GROUP, GROUP), 1, hd)

    def chunk(ci, carry):
        sl = pl.ds(pl.multiple_of(ci * c, c), c)
        rc, kc, vc, ac, bc, lwc = r_s[sl, :], k_s[sl, :], v_s[sl, :], a_s[sl, :], b_s[sl, :], lw_s[sl, :]
        cum = _dot_sel_left(tri, lwc)
        last = cum[c - 1:c, :]
        e_inv = jnp.exp(-cum)
        e_end = jnp.exp(last - cum)
        at = ac * jnp.exp(cum - lwc)
        rt = rc * jnp.exp(cum)
        bt = bc * e_inv
        kt = kc * e_inv
        lhs = jnp.concatenate([at, rt], axis=0)
        xb = _dot_nt(lhs, bd(bt))
        xk = _dot_nt(lhs, bd(kt))
        aab = jnp.where(strict, xb[:c], 0.0)
        prb = jnp.where(incl, xb[c:], 0.0)
        aak = jnp.where(strict, xk[:c], 0.0)
        prk = jnp.where(incl, xk[c:], 0.0)
        tm = _neumann_inverse(aab, eye_l, hd)
        xv = _dot(jnp.concatenate([aak, prk], axis=0), bd(vc))
        wmat = _dot(tm, bd(at))
        u0 = _dot(tm, bd(xv[:c]))
        ht = ht_ref[...]
        xh = _dot_nt(jnp.concatenate([wmat, rt], axis=0), ht)
        u = xh[:c] + u0
        y_s[sl, :] = xh[c:] + _dot(prb, bd(u)) + xv[c:]
        upd = _dot_tn(jnp.concatenate([u, vc], axis=0),
                      jnp.concatenate([bc * e_end, kc * e_end], axis=0))
        ht_ref[...] = jnp.exp(last) * ht + jnp.where(bd_mask, upd, 0.0)
        return carry

    lax.fori_loop(0, n_chunks, chunk, 0)

    y = y_s[...]
    inv_n = 1.0 / hd
    mean = segsum(y) * inv_n
    yc = y - mean
    var = segsum(yc * yc) * inv_n
    yn = yc * lax.rsqrt(var + RW_LNX_EPS) * lnx_g + lnx_b
    bonus = segsum(r * k2 * r_k) * v
    z_ref[...] = ((yn + bonus) * gate).astype(z_ref.dtype)


def _rwkv_chunk(rkv, v_first, lh, wup, g2, vecs, batch, seq_len, has_vres):
    _, m, d = rkv.shape
    lhw = lh.shape[1]
    tb = _pick(seq_len, (256, 128, 64))
    nt = seq_len // tb
    groups = d // GROUP
    row = lambda b, j, t: b * nt + t
    plane = lambda p: pl.BlockSpec((None, tb, GROUP), lambda b, j, t: (p, row(b, j, t), j))
    return pl.pallas_call(
        functools.partial(_rwkv_chunk_kernel, has_vres=has_vres, n_chunks=tb // CHUNK),
        grid=(batch, groups, nt),
        in_specs=[plane(0), plane(1), plane(2), plane(2),
                  pl.BlockSpec((tb, lhw), lambda b, j, t: (row(b, j, t), 0)),
                  pl.BlockSpec((3, V7X_LANES, GROUP), lambda b, j, t: (0, 0, j)),
                  pl.BlockSpec((g2.shape[0], GROUP), lambda b, j, t: (0, j)),
                  pl.BlockSpec((V7X_SUBLANES, GROUP), lambda b, j, t: (0, j))],
        out_specs=pl.BlockSpec((tb, GROUP), lambda b, j, t: (row(b, j, t), j)),
        out_shape=jax.ShapeDtypeStruct((m, d), BF16),
        scratch_shapes=[pltpu.VMEM((GROUP, GROUP), F32)] + [pltpu.VMEM((tb, GROUP), F32)] * 7,
        compiler_params=_params("parallel", "parallel", "arbitrary"),
        name="rwkv_chunk",
    )(rkv, rkv, rkv, v_first, lh, wup, g2, vecs)


def _gdn_gates_kernel(x_ref, w_ref, wt_ref, ad_ref, adt_ref, gb_ref, gt_ref, *, hv, bm):
    c = CHUNK
    x = x_ref[...]
    ba = jnp.dot(x, w_ref[...], preferred_element_type=F32)
    bat = lax.dot_general(wt_ref[...], x, (((1,), (1,)), ((), ())),
                          preferred_element_type=F32)
    ad = ad_ref[...]
    adt = adt_ref[...]
    g = -jnp.exp(ad[0:1, :]) * jax.nn.softplus(ba[:, hv:] + ad[1:2, :])
    gt = -jnp.exp(adt[:, 0:1]) * jax.nn.softplus(bat[hv:, :] + adt[:, 1:2])
    gb_ref[:, 0:hv] = jax.nn.sigmoid(ba[:, :hv])
    gt_ref[0:hv, :] = jax.nn.sigmoid(bat[:hv, :])
    gb_ref[:, hv:] = _dot_sel_left(_block_tri(bm, c, upper=False), g)
    gt_ref[hv:, :] = _dot_sel_right(gt, _block_tri(bm, c, upper=True))


def _gdn_gates(x16, w_ba, a_log, dt_bias):
    m, d = x16.shape
    hv = a_log.shape[0]
    bm = _pick(m, (512, 256, 128))
    ad = jnp.concatenate([a_log.reshape(1, hv), dt_bias.reshape(1, hv),
                          jnp.zeros((V7X_SUBLANES - 2, hv), F32)], axis=0)
    adt = jnp.transpose(ad)
    return pl.pallas_call(
        functools.partial(_gdn_gates_kernel, hv=hv, bm=bm),
        grid=(m // bm,),
        in_specs=[pl.BlockSpec((bm, d), lambda i: (i, 0)),
                  pl.BlockSpec((d, 2 * hv), lambda i: (0, 0)),
                  pl.BlockSpec((2 * hv, d), lambda i: (0, 0)),
                  pl.BlockSpec((V7X_SUBLANES, hv), lambda i: (0, 0)),
                  pl.BlockSpec((hv, V7X_SUBLANES), lambda i: (0, 0))],
        out_specs=[pl.BlockSpec((bm, 2 * hv), lambda i: (i, 0)),
                   pl.BlockSpec((2 * hv, bm), lambda i: (0, i))],
        out_shape=[jax.ShapeDtypeStruct((m, 2 * hv), F32), jax.ShapeDtypeStruct((2 * hv, m), F32)],
        compiler_params=_params("parallel"),
        name="gdn_gates",
    )(x16, w_ba, jnp.transpose(w_ba), ad, adt)


def _gdn_chunk_kernel(q_ref, k_ref, v_ref, z_ref, gb_ref, grow_ref, brow_ref, nw_ref, o_ref,
                      s_ref, q_s, k_s, gc64_s, bc64_s, gc128_s, o_s, *, hv, n_chunks):
    c = CHUNK
    hd = GDN_HEAD
    vh = 4
    j = pl.program_id(1)

    @pl.when(pl.program_id(2) == 0)
    def _():
        s_ref[...] = jnp.zeros(s_ref.shape, F32)

    ones_blk = _block_ones(2 * hd, hd)
    segsum = lambda t: _segsum(t, ones_blk)
    l2 = lambda t: t * lax.rsqrt(segsum(t * t) + NORM_EPS)
    q_s[...] = l2(q_ref[...]) * (hd ** -0.5)
    k_s[...] = l2(k_ref[...])

    gb = gb_ref[...]
    src = _iota((2 * hv, vh * c), 0)
    sel64 = lambda base: jnp.where(src == base + vh * j + _seg((2 * hv, vh * c), 1, c),
                                   1.0, 0.0).astype(BF16)
    src2 = _iota((2 * hv, vh * hd), 0)
    sel128 = jnp.where(src2 == hv + vh * j + _seg((2 * hv, vh * hd), 1, hd), 1.0, 0.0).astype(BF16)
    bc64_s[...] = _dot_sel_right(gb, sel64(0))
    gc64_s[...] = _dot_sel_right(gb, sel64(hv))
    gc128_s[...] = _dot_sel_right(gb, sel128)

    row = _iota((c, vh * c), 0)
    col = _pos((c, vh * c), 1, c)
    strict = row > col
    incl = row >= col
    eye_l = jnp.where(row == col, 1.0, 0.0).astype(F32)
    kmask = _seg((vh * c, 2 * hd), 0, 2 * c) == _seg((vh * c, 2 * hd), 1, hd)
    pair_mask = _seg((2 * hd, 2 * hd), 0, hd) == _seg((2 * hd, 2 * hd), 1, hd)
    rep = lambda t: jnp.concatenate([t[:, :hd], t[:, :hd], t[:, hd:], t[:, hd:]], axis=1)
    bd = lambda t: _block_diag(t, c, hd)

    def chunk(ci, carry):
        sl = pl.ds(pl.multiple_of(ci * c, c), c)
        qc, kc, vc = q_s[sl, :], k_s[sl, :], v_ref[sl, :]
        gcol, bcol, gcol128 = gc64_s[sl, :], bc64_s[sl, :], gc128_s[sl, :]
        grow = grow_ref[pl.ds(ci, 1), :]
        brow = brow_ref[pl.ds(ci, 1), :]
        decay = jnp.exp(jnp.where(incl, gcol - grow, 0.0))
        k_rows = jnp.where(kmask, jnp.concatenate([kc] * vh, axis=0), 0.0)
        both = _dot_nt(jnp.concatenate([kc, qc], axis=0), k_rows)
        amat = jnp.where(strict, both[:c] * decay * bcol, 0.0)
        attn = jnp.where(incl, both[c:] * decay, 0.0)
        tm = _neumann_inverse(-amat, eye_l, c)
        k4 = rep(kc)
        u = _dot(tm * brow, bd(vc))
        wk = _dot(tm * (brow * jnp.exp(grow)), bd(k4))
        q4 = rep(qc)
        e_g = jnp.exp(gcol128)
        glast = gcol128[c - 1:c, :]
        e_last = jnp.exp(glast)
        outs = []
        vns = []
        for p in range(2):
            lanes = slice(p * 2 * hd, (p + 1) * 2 * hd)
            s_bd = s_ref[p]
            ws = _dot(jnp.concatenate([wk[:, lanes], q4[:, lanes]], axis=0), s_bd)
            vns.append(u[:, lanes] - ws[:c])
            outs.append(ws[c:])
        v_new = jnp.concatenate(vns, axis=1)
        o_s[sl, :] = jnp.concatenate(outs, axis=1) * e_g + _dot(attn, bd(v_new))
        vsc = v_new * jnp.exp(glast - gcol128)
        for p in range(2):
            lanes = slice(p * 2 * hd, (p + 1) * 2 * hd)
            upd = _dot_tn(k4[:, lanes], vsc[:, lanes])
            s_ref[p] = s_ref[p] * e_last[:, lanes] + jnp.where(pair_mask, upd, 0.0)
        return carry

    lax.fori_loop(0, n_chunks, chunk, 0)

    o = o_s[...]
    ones4 = _block_ones(vh * hd, hd)
    ms = _segsum(o * o, ones4) * (1.0 / hd)
    z = z_ref[...]
    o_ref[...] = (o * lax.rsqrt(ms + NORM_EPS) * nw_ref[0:1, :] * (z * jax.nn.sigmoid(z))).astype(o_ref.dtype)


def _gdn_chunk(qkv, zed, gates, g_rows, b_rows, norm_w4, batch, seq_len, key_dim, value_dim):
    m = qkv.shape[0]
    hv = gates.shape[1] // 2
    hd = GDN_HEAD
    tb = _pick(seq_len, (512,))
    nt = seq_len // tb
    groups = value_dim // (4 * hd)
    kb = key_dim // (2 * hd)
    vb = 2 * key_dim // (4 * hd)
    row = lambda b, j, t: b * nt + t
    rows_per_tb = tb // CHUNK
    return pl.pallas_call(
        functools.partial(_gdn_chunk_kernel, hv=hv, n_chunks=tb // CHUNK),
        grid=(batch, groups, nt),
        in_specs=[pl.BlockSpec((tb, 2 * hd), lambda b, j, t: (row(b, j, t), j)),
                  pl.BlockSpec((tb, 2 * hd), lambda b, j, t: (row(b, j, t), kb + j)),
                  pl.BlockSpec((tb, 4 * hd), lambda b, j, t: (row(b, j, t), vb + j)),
                  pl.BlockSpec((tb, 4 * hd), lambda b, j, t: (row(b, j, t), j)),
                  pl.BlockSpec((tb, 2 * hv), lambda b, j, t: (row(b, j, t), 0)),
                  pl.BlockSpec((rows_per_tb, 4 * CHUNK), lambda b, j, t: (row(b, j, t), j)),
                  pl.BlockSpec((rows_per_tb, 4 * CHUNK), lambda b, j, t: (row(b, j, t), j)),
                  pl.BlockSpec((V7X_SUBLANES, 4 * hd), lambda b, j, t: (0, 0))],
        out_specs=pl.BlockSpec((tb, 4 * hd), lambda b, j, t: (row(b, j, t), j)),
        out_shape=jax.ShapeDtypeStruct((m, value_dim), BF16),
        scratch_shapes=[pltpu.VMEM((2, 2 * hd, 2 * hd), F32),
                        pltpu.VMEM((tb, 2 * hd), F32), pltpu.VMEM((tb, 2 * hd), F32),
                        pltpu.VMEM((tb, 4 * CHUNK), F32), pltpu.VMEM((tb, 4 * CHUNK), F32),
                        pltpu.VMEM((tb, 4 * hd), F32), pltpu.VMEM((tb, 4 * hd), F32)],
        compiler_params=_params("parallel", "parallel", "arbitrary"),
        name="gdn_chunk",
    )(qkv, qkv, qkv, zed, gates, g_rows, b_rows, norm_w4)


def _pad_rows(w, rows):
    return jnp.pad(w, ((0, rows - w.shape[0]), (0, 0)))


def _pad_cols(w, cols):
    return jnp.pad(w, ((0, 0), (0, cols - w.shape[1])))


def _rwkv_layer(x32, v_first, p, batch, seq_len, alpha, ln_g, ln_b):
    m, d = x32.shape
    lp = V7X_LANES
    has_vres = v_first is not None
    mu8 = _pad_rows(p["mu"], V7X_SUBLANES)
    v1 = p["v1"] if has_vres else jnp.zeros((d, lp), F32)
    wd = jnp.concatenate([_pad_cols(p["w1"], lp), _pad_cols(p["a1"], lp), _pad_cols(v1, lp), p["g1"]],
                         axis=1).astype(BF16)
    xm, lh = _rwkv_in(x32, mu8, wd, seq_len, has_vres)
    ws = jnp.stack([p["w_r"], p["w_k"], p["w_v"]]).astype(BF16)
    rkv = _matmul_stack(xm, ws, F32)
    v2 = p["v2"] if has_vres else jnp.zeros((lp, d), F32)
    wup = jnp.stack([_pad_rows(p["w2"], lp), _pad_rows(p["a2"], lp), _pad_rows(v2, lp)]).astype(BF16)
    v0 = p["v0"] if has_vres else jnp.zeros((d,), F32)
    vecs = jnp.stack([p["w0"], p["a0"], v0, p["k_k"], p["k_a"], p["r_k"].reshape(d), p["lnx_g"], p["lnx_b"]])
    vf = v_first if has_vres else rkv
    z = _rwkv_chunk(rkv, vf, lh, wup, p["g2"].astype(BF16), vecs, batch, seq_len, has_vres)
    x32, x16 = _matmul_res_ln(z, p["w_o"].astype(BF16), x32, ln_g, ln_b, alpha)
    return x32, x16, vf


def _gdn_layer(x32, x16, p, batch, seq_len, alpha, ln_g, ln_b):
    m, d = x32.shape
    hv = p["a_log"].shape[0]
    value_dim = p["w_out"].shape[0]
    conv_ch = p["conv_w"].shape[1]
    key_dim = (conv_ch - value_dim) // 2
    w_in = p["w_in"]
    taps = p["conv_w"].shape[0]
    qkv = _matmul_conv(x16, w_in[:, :conv_ch].astype(BF16), _conv_table(p["conv_w"], None, conv_ch),
                       seq_len, taps, F32)
    zed = _matmul_stack(x16[None], w_in[None, :, conv_ch:conv_ch + value_dim].astype(BF16), F32)[0]
    gates, gates_t = _gdn_gates(x16, w_in[:, conv_ch + value_dim:].astype(BF16), p["a_log"], p["dt_bias"])
    rows = lambda t: t.reshape(hv, m // CHUNK, CHUNK).transpose(1, 0, 2).reshape(m // CHUNK, hv * CHUNK)
    b_rows = rows(gates_t[:hv])
    g_rows = rows(gates_t[hv:])
    norm_w4 = jnp.tile(_pad_rows(p["norm_w"].reshape(1, GDN_HEAD), V7X_SUBLANES), (1, 4))
    o = _gdn_chunk(qkv, zed, gates, g_rows, b_rows, norm_w4, batch, seq_len, key_dim, value_dim)
    return _matmul_res_ln(o, p["w_out"].astype(BF16), x32, ln_g, ln_b, alpha)


def _ffn_layer(x32, x16, p, seq_len, alpha, ln_g, ln_b):
    d_ff = p["w_down"].shape[0]
    ffp = _round_up(d_ff, 512 if d_ff >= 2048 else V7X_LANES)
    w_up, conv_w, conv_b = p["w_up"], p["conv_w"], p["conv_b"]
    wg = _pad_cols(w_up[:, :d_ff], ffp).astype(BF16)
    wu = _pad_cols(w_up[:, d_ff:], ffp).astype(BF16)
    tab_g = _conv_table(conv_w[:, :d_ff], conv_b[:d_ff], ffp)
    tab_u = _conv_table(conv_w[:, d_ff:], conv_b[d_ff:], ffp)
    act = _matmul_conv(x16, wg, tab_g, seq_len, conv_w.shape[0], BF16, w_up=wu, tab_up=tab_u)
    w_down = _pad_rows(p["w_down"], ffp).astype(BF16)
    return _matmul_res_ln(act, w_down, x32, ln_g, ln_b, alpha)


def kernel(x, rw_mu, rw_w_r, rw_w_k, rw_w_v, rw_w_o, rw_w0, rw_w1, rw_w2, rw_a0, rw_a1, rw_a2, rw_v0, rw_v1, rw_v2, rw_g1, rw_g2, rw_k_k, rw_k_a, rw_r_k, rw_lnx_g, rw_lnx_b, gdn_w_in, gdn_conv_w, gdn_a_log, gdn_dt_bias, gdn_norm_w, gdn_w_out, ffn_w_up, ffn_conv_w, ffn_conv_b, ffn_w_down, ln_mix_g, ln_mix_b, ln_ffn_g, ln_ffn_b):
    batch, seq_len, d = x.shape
    depth = ln_mix_g.shape[0]
    alpha = (2 * depth) ** 0.25
    x32 = x.reshape(batch * seq_len, d)
    x16 = None
    v_first = None
    for i in range(depth):
        j = i // 2
        if i % 2 == 0:
            p = dict(mu=rw_mu[j], w_r=rw_w_r[j], w_k=rw_w_k[j], w_v=rw_w_v[j], w_o=rw_w_o[j], w0=rw_w0[j],
                     w1=rw_w1[j], w2=rw_w2[j], a0=rw_a0[j], a1=rw_a1[j], a2=rw_a2[j], g1=rw_g1[j], g2=rw_g2[j],
                     k_k=rw_k_k[j], k_a=rw_k_a[j], r_k=rw_r_k[j], lnx_g=rw_lnx_g[j], lnx_b=rw_lnx_b[j])
            if j > 0:
                p.update(v0=rw_v0[j - 1], v1=rw_v1[j - 1], v2=rw_v2[j - 1])
            x32, x16, v_first = _rwkv_layer(x32, v_first, p, batch, seq_len, alpha, ln_mix_g[i], ln_mix_b[i])
        else:
            p = dict(w_in=gdn_w_in[j], conv_w=gdn_conv_w[j], a_log=gdn_a_log[j], dt_bias=gdn_dt_bias[j],
                     norm_w=gdn_norm_w[j], w_out=gdn_w_out[j])
            x32, x16 = _gdn_layer(x32, x16, p, batch, seq_len, alpha, ln_mix_g[i], ln_mix_b[i])
        p = dict(w_up=ffn_w_up[i], conv_w=ffn_conv_w[i], conv_b=ffn_conv_b[i], w_down=ffn_w_down[i])
        x32, x16 = _ffn_layer(x32, x16, p, seq_len, alpha, ln_ffn_g[i], ln_ffn_b[i])
    return x32.reshape(batch, seq_len, d)
```

```python
import functools
import math

import jax
import jax.numpy as jnp
from jax import lax
from jax.experimental import pallas as pl
from jax.experimental.pallas import tpu as pltpu

F32 = jnp.float32
BF16 = jnp.bfloat16

RW_HEAD = 64
GDN_HEAD = 128
LN_EPS = 1e-5
NORM_EPS = 1e-6
RW_LNX_EPS = 64e-5

V7X_LANES = 128
V7X_SUBLANES = 8
V7X_MXU_DIM = 256
V7X_VMEM_BYTES = 64 * 1024 * 1024
VMEM_LIMIT = V7X_VMEM_BYTES * 7 // 8

CHUNK = 64
GROUP = V7X_MXU_DIM
HALO = V7X_SUBLANES


def _params(*sem):
    return pltpu.CompilerParams(dimension_semantics=sem, vmem_limit_bytes=VMEM_LIMIT)


def _pick(n, prefs):
    for p in prefs:
        if p <= n and n % p == 0:
            return p
    return n


def _round_up(n, m):
    return (n + m - 1) // m * m


def _dot(a, b):
    return jnp.dot(a.astype(BF16), b.astype(BF16), preferred_element_type=F32)


def _dot_nt(a, b):
    return lax.dot_general(a.astype(BF16), b.astype(BF16), (((1,), (1,)), ((), ())),
                           preferred_element_type=F32)


def _dot_tn(a, b):
    return lax.dot_general(a.astype(BF16), b.astype(BF16), (((0,), (0,)), ((), ())),
                           preferred_element_type=F32)


def _split(x, terms):
    out = []
    rest = x
    for _ in range(terms):
        t = rest.astype(BF16)
        out.append(t)
        rest = rest - t.astype(F32)
    return out


def _dot_sel_left(sel, x, terms=3):
    acc = None
    for t in _split(x, terms):
        p = jnp.dot(sel, t, preferred_element_type=F32)
        acc = p if acc is None else acc + p
    return acc


def _dot_sel_right(x, sel, terms=3):
    acc = None
    for t in _split(x, terms):
        p = jnp.dot(t, sel, preferred_element_type=F32)
        acc = p if acc is None else acc + p
    return acc


def _iota(shape, dim):
    return lax.broadcasted_iota(jnp.int32, shape, dim)


def _seg(shape, dim, seg):
    return lax.shift_right_logical(_iota(shape, dim), int(math.log2(seg)))


def _pos(shape, dim, seg):
    return lax.bitwise_and(_iota(shape, dim), seg - 1)


def _block_ones(n, seg):
    return jnp.where(_seg((n, n), 0, seg) == _seg((n, n), 1, seg), 1.0, 0.0).astype(BF16)


def _block_tri(n, seg, upper):
    r, c = _iota((n, n), 0), _iota((n, n), 1)
    same = _seg((n, n), 0, seg) == _seg((n, n), 1, seg)
    tri = (r <= c) if upper else (r >= c)
    return jnp.where(jnp.logical_and(same, tri), 1.0, 0.0).astype(BF16)


def _segsum(x, ones_blk):
    return _dot_sel_right(x, ones_blk, terms=2)


def _bd_mask(rows, lanes, row_seg, lane_seg):
    return _seg((rows, lanes), 0, row_seg) == _seg((rows, lanes), 1, lane_seg)


def _block_diag(x, mask):
    x16 = x.astype(BF16)
    tall = jnp.concatenate([x16] * (mask.shape[0] // x.shape[0]), axis=0)
    return jnp.where(mask, tall, jnp.zeros_like(tall))


def _layer_norm(y, g, b):
    mu = jnp.mean(y, axis=-1, keepdims=True)
    yc = y - mu
    var = jnp.mean(yc * yc, axis=-1, keepdims=True)
    return yc * lax.rsqrt(var + LN_EPS) * g + b


def _mm_kernel(x_ref, w_ref, o_ref):
    o_ref[...] = jnp.dot(x_ref[...], w_ref[...], preferred_element_type=F32).astype(o_ref.dtype)


def _matmul_stack(xs, ws, out_dtype):
    p, m, k = xs.shape
    n = ws.shape[2]
    bm = _pick(m, (1024, 512, 256, 128))
    bn = _pick(n, (1024, 512, 256, 128))
    return pl.pallas_call(
        _mm_kernel,
        grid=(p, n // bn, m // bm),
        in_specs=[pl.BlockSpec((None, bm, k), lambda q, j, i: (q, i, 0)),
                  pl.BlockSpec((None, k, bn), lambda q, j, i: (q, 0, j))],
        out_specs=pl.BlockSpec((None, bm, bn), lambda q, j, i: (q, i, j)),
        out_shape=jax.ShapeDtypeStruct((p, m, n), out_dtype),
        compiler_params=_params("parallel", "parallel", "parallel"),
        name="matmul_stack",
    )(xs, ws)


def _mm_res_ln_kernel(a_ref, w_ref, res_ref, gb_ref, o32_ref, o16_ref, acc_ref, *, alpha, nk):
    kk = pl.program_id(1)
    part = jnp.dot(a_ref[...], w_ref[...], preferred_element_type=F32)

    @pl.when(kk == 0)
    def _():
        acc_ref[...] = part

    @pl.when(kk > 0)
    def _():
        acc_ref[...] += part

    @pl.when(kk == nk - 1)
    def _():
        y = alpha * res_ref[...] + acc_ref[...]
        out = _layer_norm(y, gb_ref[0:1, :], gb_ref[1:2, :])
        o32_ref[...] = out
        o16_ref[...] = out.astype(BF16)


def _matmul_res_ln(a, w, resid, gamma, beta, alpha):
    m, k = a.shape
    d = w.shape[1]
    bm = _pick(m, (512, 256, 128))
    bk = _pick(k, (1408, 1024, 512, 256, 128))
    nk = k // bk
    gb = jnp.concatenate([gamma.reshape(1, d), beta.reshape(1, d),
                          jnp.zeros((V7X_SUBLANES - 2, d), F32)], axis=0)
    return pl.pallas_call(
        functools.partial(_mm_res_ln_kernel, alpha=alpha, nk=nk),
        grid=(m // bm, nk),
        in_specs=[pl.BlockSpec((bm, bk), lambda i, kk: (i, kk)),
                  pl.BlockSpec((bk, d), lambda i, kk: (kk, 0)),
                  pl.BlockSpec((bm, d), lambda i, kk: (i, 0)),
                  pl.BlockSpec((V7X_SUBLANES, d), lambda i, kk: (0, 0))],
        out_specs=[pl.BlockSpec((bm, d), lambda i, kk: (i, 0)),
                   pl.BlockSpec((bm, d), lambda i, kk: (i, 0))],
        out_shape=[jax.ShapeDtypeStruct((m, d), F32), jax.ShapeDtypeStruct((m, d), BF16)],
        scratch_shapes=[pltpu.VMEM((bm, d), F32)],
        compiler_params=_params("parallel", "arbitrary"),
        name="matmul_res_ln",
    )(a, w, resid, gb)


def _causal_conv(h, hh, cw, taps, buf_ref):
    bm = h.shape[0]
    buf_ref[0:HALO, :] = hh
    buf_ref[HALO:HALO + bm, :] = h
    out = cw[taps - 1:taps, :] * h
    for k in range(taps - 1):
        shift = taps - 1 - k
        out = out + cw[k:k + 1, :] * buf_ref[HALO - shift:HALO - shift + bm, :]
    return out


def _mm_conv_kernel(x_ref, xh_ref, *refs, taps, glu, seq_len, bm):
    if glu:
        wg_ref, wu_ref, cg_ref, cu_ref, o_ref, buf_ref = refs
    else:
        wg_ref, cg_ref, o_ref, buf_ref = refs
    i = pl.program_id(1)
    valid = jnp.where((i * bm) % seq_len == 0, 0.0, 1.0).astype(F32)
    x = x_ref[...]
    xh = xh_ref[...]

    def branch(w_ref, c_ref):
        h = jnp.dot(x, w_ref[...], preferred_element_type=F32)
        hh = jnp.dot(xh, w_ref[...], preferred_element_type=F32) * valid
        c = c_ref[...]
        return _causal_conv(h, hh, c, taps, buf_ref) + c[taps:taps + 1, :]

    gate = branch(wg_ref, cg_ref)
    act = gate * jax.nn.sigmoid(gate)
    if glu:
        act = act * branch(wu_ref, cu_ref)
    o_ref[...] = act.astype(o_ref.dtype)


def _conv_table(conv_w, bias, n_pad):
    taps, n = conv_w.shape
    rows = [conv_w, (jnp.zeros((1, n), F32) if bias is None else bias.reshape(1, n)),
            jnp.zeros((V7X_SUBLANES - taps - 1, n), F32)]
    tab = jnp.concatenate(rows, axis=0)
    return jnp.pad(tab, ((0, 0), (0, n_pad - n)))


def _matmul_conv(x16, w, tab, seq_len, taps, out_dtype, w_up=None, tab_up=None):
    m, k = x16.shape
    n = w.shape[1]
    glu = w_up is not None
    bm = _pick(seq_len, (1024, 512, 256, 128, 64, 32, 16, 8))
    bn = _pick(n, (512, 256, 128))
    hb = bm // HALO
    x_spec = pl.BlockSpec((bm, k), lambda j, i: (i, 0))
    xh_spec = pl.BlockSpec((HALO, k), lambda j, i: (jnp.maximum(i * hb - 1, 0), 0))
    w_spec = pl.BlockSpec((k, bn), lambda j, i: (0, j))
    c_spec = pl.BlockSpec((V7X_SUBLANES, bn), lambda j, i: (0, j))
    if glu:
        in_specs = [x_spec, xh_spec, w_spec, w_spec, c_spec, c_spec]
        args = (x16, x16, w, w_up, tab, tab_up)
    else:
        in_specs = [x_spec, xh_spec, w_spec, c_spec]
        args = (x16, x16, w, tab)
    return pl.pallas_call(
        functools.partial(_mm_conv_kernel, taps=taps, glu=glu, seq_len=seq_len, bm=bm),
        grid=(n // bn, m // bm),
        in_specs=in_specs,
        out_specs=pl.BlockSpec((bm, bn), lambda j, i: (i, j)),
        out_shape=jax.ShapeDtypeStruct((m, n), out_dtype),
        scratch_shapes=[pltpu.VMEM((bm + HALO, bn), F32)],
        compiler_params=_params("parallel", "parallel"),
        name="matmul_conv_glu" if glu else "matmul_conv",
    )(*args)


def _rwkv_in_kernel(x_ref, xh_ref, mu_ref, wd_ref, xm_ref, lh_ref, *, seq_len, bm, has_vres):
    i = pl.program_id(0)
    valid = jnp.where((i * bm) % seq_len == 0, 0.0, 1.0).astype(F32)
    x = x_ref[...]
    prev_row = xh_ref[HALO - 1:HALO, :] * valid
    rolled = pltpu.roll(x, shift=1, axis=0)
    x_prev = jnp.where(_iota(x.shape, 0) == 0, prev_row, rolled)
    xx = x_prev - x
    mu = mu_ref[...]
    mix = lambda r: (x + xx * mu[r:r + 1, :]).astype(BF16)
    xm_ref[0] = mix(0)
    xm_ref[1] = mix(2)
    xv = mix(3)
    xm_ref[2] = xv
    lp = V7X_LANES
    down = lambda xin, lo, hi: jnp.dot(xin, wd_ref[:, lo:hi], preferred_element_type=F32)
    lh_ref[:, 0:lp] = jnp.tanh(down(mix(1), 0, lp)).astype(BF16)
    lh_ref[:, lp:2 * lp] = down(mix(4), lp, 2 * lp).astype(BF16)
    if has_vres:
        lh_ref[:, 2 * lp:3 * lp] = down(xv, 2 * lp, 3 * lp).astype(BF16)
    else:
        lh_ref[:, 2 * lp:3 * lp] = jnp.zeros((bm, lp), BF16)
    lh_ref[:, 3 * lp:] = jax.nn.sigmoid(down(mix(5), 3 * lp, wd_ref.shape[1])).astype(BF16)


def _rwkv_in(x, mu8, wd, seq_len, has_vres):
    m, d = x.shape
    lh = wd.shape[1]
    bm = _pick(seq_len, (512, 256, 128, 64, 32, 16, 8))
    hb = bm // HALO
    return pl.pallas_call(
        functools.partial(_rwkv_in_kernel, seq_len=seq_len, bm=bm, has_vres=has_vres),
        grid=(m // bm,),
        in_specs=[pl.BlockSpec((bm, d), lambda i: (i, 0)),
                  pl.BlockSpec((HALO, d), lambda i: (jnp.maximum(i * hb - 1, 0), 0)),
                  pl.BlockSpec((V7X_SUBLANES, d), lambda i: (0, 0)),
                  pl.BlockSpec((d, lh), lambda i: (0, 0))],
        out_specs=[pl.BlockSpec((3, bm, d), lambda i: (0, i, 0)),
                   pl.BlockSpec((bm, lh), lambda i: (i, 0))],
        out_shape=[jax.ShapeDtypeStruct((3, m, d), BF16), jax.ShapeDtypeStruct((m, lh), BF16)],
        compiler_params=_params("parallel"),
        name="rwkv_in",
    )(x, x, mu8, wd)


def _lockstep(chains):
    chains = list(chains)
    while chains:
        alive = []
        for ch in chains:
            try:
                next(ch)
                alive.append(ch)
            except StopIteration:
                pass
        chains = alive


def _neumann_chain(nmat, eye_l, bd, out):
    c = nmat.shape[0]
    p = eye_l + nmat
    nk = _dot(nmat, bd(nmat))
    yield
    n = 2
    while 2 * n < c:
        both = _dot(jnp.concatenate([p, nk], axis=0), bd(nk))
        yield
        p = p + both[:c]
        nk = both[c:]
        n *= 2
    out["tm"] = p + _dot(p, bd(nk))
    yield


def _rwkv_chunk_kernel(r_ref, k_ref, v_ref, vf_ref, lh_ref, wup_ref, g2_ref, vec_ref, z_ref,
                       ht_ref, *, has_vres, n_chunks, n_groups):
    c = CHUNK
    hd = RW_HEAD
    lp = V7X_LANES

    @pl.when(pl.program_id(2) == 0)
    def _():
        ht_ref[...] = jnp.zeros(ht_ref.shape, F32)

    ones_blk = _block_ones(GROUP, hd)
    segsum = lambda t: _segsum(t, ones_blk)
    row = _iota((c, GROUP), 0)
    col = _pos((c, GROUP), 1, c)
    strict = row > col
    incl = row >= col
    eye_l = jnp.where(row == col, 1.0, 0.0).astype(F32)
    mask = _bd_mask(GROUP, GROUP, c, hd)
    bd = lambda t: _block_diag(t, mask)
    tri = _block_tri(n_chunks * c, c, upper=False)

    grp = []
    for g in range(n_groups):
        ln = slice(g * GROUP, (g + 1) * GROUP)
        vec = vec_ref[:, ln]
        w0, a0, v0, k_k, k_a, r_k, lnx_g, lnx_b = [vec[i:i + 1, :] for i in range(8)]
        r = r_ref[:, ln]
        k = k_ref[:, ln]
        v = v_ref[:, ln]
        up = lambda lo, hi, w: jnp.dot(lh_ref[:, lo:hi], w, preferred_element_type=F32)
        w = -jax.nn.softplus(-(w0 + up(0, lp, wup_ref[0, :, ln]))) - 0.5
        a = jax.nn.sigmoid(a0 + up(lp, 2 * lp, wup_ref[1, :, ln]))
        if has_vres:
            v = v + (vf_ref[:, ln] - v) * jax.nn.sigmoid(v0 + up(2 * lp, 3 * lp, wup_ref[2, :, ln]))
        gate = up(3 * lp, lh_ref.shape[1], g2_ref[:, ln])
        kk = k * k_k
        kk = kk * lax.rsqrt(segsum(kk * kk) + NORM_EPS)
        k2 = k * (1.0 + (a - 1.0) * k_a)
        bv = kk * a
        lw = -jnp.exp(w)
        cum = _dot_sel_left(tri, lw)
        e_inv = jnp.exp(-cum)
        grp.append(dict(r=r, k2=k2, v=v, bv=bv, cum=cum, gate=gate, r_k=r_k, lnx_g=lnx_g, lnx_b=lnx_b,
                        at=-kk * jnp.exp(cum - lw), rt=r * jnp.exp(cum), bt=bv * e_inv, kt=k2 * e_inv))

    res = {}

    def prepare(g, ci):
        d = grp[g]
        sl = slice(ci * c, (ci + 1) * c)
        vc, at, rt = d["v"][sl], d["at"][sl], d["rt"][sl]
        last = d["cum"][(ci + 1) * c - 1:(ci + 1) * c, :]
        e_end = jnp.exp(last - d["cum"][sl])
        b_end = d["bv"][sl] * e_end
        k_end = d["k2"][sl] * e_end
        lhs = jnp.concatenate([at, rt], axis=0)
        xb = _dot_nt(lhs, bd(d["bt"][sl]))
        xk = _dot_nt(lhs, bd(d["kt"][sl]))
        yield
        aab = jnp.where(strict, xb[:c], 0.0)
        prb = jnp.where(incl, xb[c:], 0.0)
        aak = jnp.where(strict, xk[:c], 0.0)
        prk = jnp.where(incl, xk[c:], 0.0)
        xv = _dot(jnp.concatenate([aak, prk], axis=0), bd(vc))
        inv = {}
        yield from _neumann_chain(aab, eye_l, bd, inv)
        tm = inv["tm"]
        wmat = _dot(tm, bd(at))
        u0 = _dot(tm, bd(xv[:c]))
        yield
        q_bd = jnp.where(mask, _dot_tn(wmat, b_end), 0.0)
        n_bd = jnp.where(mask, _dot_tn(jnp.concatenate([u0, vc], axis=0),
                                       jnp.concatenate([b_end, k_end], axis=0)), 0.0)
        yield
        res[g, ci] = dict(lhs=jnp.concatenate([wmat, rt], axis=0), u0=u0, prb=prb, y0=xv[c:],
                          q_bd=q_bd, n_bd=n_bd, decay=jnp.exp(last))

    ys = [[] for _ in range(n_groups)]

    def advance(g):
        ht = ht_ref[g]
        for ci in range(n_chunks):
            d = res[g, ci]
            xh = _dot_nt(d["lhs"], ht)
            ht = d["decay"] * ht + _dot(ht, d["q_bd"]) + d["n_bd"]
            yield
            u = xh[:c] + d["u0"]
            ys[g].append(xh[c:] + _dot(d["prb"], bd(u)) + d["y0"])
            yield
        ht_ref[g] = ht

    _lockstep(prepare(g, ci) for ci in range(n_chunks) for g in range(n_groups))
    _lockstep(advance(g) for g in range(n_groups))

    inv_n = 1.0 / hd
    for g in range(n_groups):
        d = grp[g]
        y = jnp.concatenate(ys[g], axis=0)
        mean = segsum(y) * inv_n
        yc = y - mean
        var = segsum(yc * yc) * inv_n
        yn = yc * lax.rsqrt(var + RW_LNX_EPS) * d["lnx_g"] + d["lnx_b"]
        bonus = segsum(d["r"] * d["k2"] * d["r_k"]) * d["v"]
        z_ref[:, g * GROUP:(g + 1) * GROUP] = ((yn + bonus) * d["gate"]).astype(z_ref.dtype)


def _rwkv_chunk(rkv, v_first, lh, wup, g2, vecs, batch, seq_len, has_vres):
    _, m, d = rkv.shape
    lhw = lh.shape[1]
    tb = _pick(seq_len, (256, 128, 64))
    nt = seq_len // tb
    n_groups = 2 if d % (2 * GROUP) == 0 else 1
    lanes = n_groups * GROUP
    row = lambda b, j, t: b * nt + t
    plane = lambda p: pl.BlockSpec((None, tb, lanes), lambda b, j, t: (p, row(b, j, t), j))
    return pl.pallas_call(
        functools.partial(_rwkv_chunk_kernel, has_vres=has_vres, n_chunks=tb // CHUNK, n_groups=n_groups),
        grid=(batch, d // lanes, nt),
        in_specs=[plane(0), plane(1), plane(2), plane(2),
                  pl.BlockSpec((tb, lhw), lambda b, j, t: (row(b, j, t), 0)),
                  pl.BlockSpec((3, V7X_LANES, lanes), lambda b, j, t: (0, 0, j)),
                  pl.BlockSpec((g2.shape[0], lanes), lambda b, j, t: (0, j)),
                  pl.BlockSpec((V7X_SUBLANES, lanes), lambda b, j, t: (0, j))],
        out_specs=pl.BlockSpec((tb, lanes), lambda b, j, t: (row(b, j, t), j)),
        out_shape=jax.ShapeDtypeStruct((m, d), BF16),
        scratch_shapes=[pltpu.VMEM((n_groups, GROUP, GROUP), F32)],
        compiler_params=_params("parallel", "parallel", "arbitrary"),
        name="rwkv_chunk",
    )(rkv, rkv, rkv, v_first, lh, wup, g2, vecs)


def _gdn_gates_kernel(x_ref, w_ref, wt_ref, ad_ref, adt_ref, gb_ref, gt_ref, *, hv, bm):
    c = CHUNK
    x = x_ref[...]
    ba = jnp.dot(x, w_ref[...], preferred_element_type=F32)
    bat = lax.dot_general(wt_ref[...], x, (((1,), (1,)), ((), ())),
                          preferred_element_type=F32)
    ad = ad_ref[...]
    adt = adt_ref[...]
    g = -jnp.exp(ad[0:1, :]) * jax.nn.softplus(ba[:, hv:] + ad[1:2, :])
    gt = -jnp.exp(adt[:, 0:1]) * jax.nn.softplus(bat[hv:, :] + adt[:, 1:2])
    gb_ref[:, 0:hv] = jax.nn.sigmoid(ba[:, :hv])
    gt_ref[0:hv, :] = jax.nn.sigmoid(bat[:hv, :])
    gb_ref[:, hv:] = _dot_sel_left(_block_tri(bm, c, upper=False), g)
    gt_ref[hv:, :] = _dot_sel_right(gt, _block_tri(bm, c, upper=True))


def _gdn_gates(x16, w_ba, a_log, dt_bias):
    m, d = x16.shape
    hv = a_log.shape[0]
    bm = _pick(m, (512, 256, 128))
    ad = jnp.concatenate([a_log.reshape(1, hv), dt_bias.reshape(1, hv),
                          jnp.zeros((V7X_SUBLANES - 2, hv), F32)], axis=0)
    adt = jnp.transpose(ad)
    return pl.pallas_call(
        functools.partial(_gdn_gates_kernel, hv=hv, bm=bm),
        grid=(m // bm,),
        in_specs=[pl.BlockSpec((bm, d), lambda i: (i, 0)),
                  pl.BlockSpec((d, 2 * hv), lambda i: (0, 0)),
                  pl.BlockSpec((2 * hv, d), lambda i: (0, 0)),
                  pl.BlockSpec((V7X_SUBLANES, hv), lambda i: (0, 0)),
                  pl.BlockSpec((hv, V7X_SUBLANES), lambda i: (0, 0))],
        out_specs=[pl.BlockSpec((bm, 2 * hv), lambda i: (i, 0)),
                   pl.BlockSpec((2 * hv, bm), lambda i: (0, i))],
        out_shape=[jax.ShapeDtypeStruct((m, 2 * hv), F32), jax.ShapeDtypeStruct((2 * hv, m), F32)],
        compiler_params=_params("parallel"),
        name="gdn_gates",
    )(x16, w_ba, jnp.transpose(w_ba), ad, adt)


def _gdn_chunk_kernel(q_ref, k_ref, v_ref, z_ref, gb_ref, grow_ref, brow_ref, nw_ref, o_ref,
                      s_ref, *, hv, n_chunks, n_groups):
    c = CHUNK
    hd = GDN_HEAD
    vh = 4
    j = pl.program_id(1)

    @pl.when(pl.program_id(2) == 0)
    def _():
        s_ref[...] = jnp.zeros(s_ref.shape, F32)

    ones_blk = _block_ones(2 * hd, hd)
    segsum = lambda t: _segsum(t, ones_blk)
    l2 = lambda t: t * lax.rsqrt(segsum(t * t) + NORM_EPS)
    row = _iota((c, vh * c), 0)
    col = _pos((c, vh * c), 1, c)
    strict = row > col
    incl = row >= col
    eye_l = jnp.where(row == col, 1.0, 0.0).astype(F32)
    kmask = _bd_mask(vh * c, 2 * hd, 2 * c, hd)
    pair_mask = _bd_mask(2 * hd, 2 * hd, hd, hd)
    cc_mask = _bd_mask(vh * c, vh * c, c, c)
    wide_mask = _bd_mask(vh * c, vh * hd, c, hd)
    rep = lambda t: jnp.concatenate([t[:, :hd], t[:, :hd], t[:, hd:], t[:, hd:]], axis=1)
    bd = lambda t: _block_diag(t, wide_mask)
    bd_cc = lambda t: _block_diag(t, cc_mask)
    pairs = [slice(p * 2 * hd, (p + 1) * 2 * hd) for p in range(2)]

    gb = gb_ref[...]
    src = _iota((2 * hv, vh * c), 0)
    src2 = _iota((2 * hv, vh * hd), 0)
    grp = []
    for g in range(n_groups):
        first = vh * (n_groups * j + g)
        sel64 = lambda base: jnp.where(src == base + first + _seg((2 * hv, vh * c), 1, c),
                                       1.0, 0.0).astype(BF16)
        sel128 = jnp.where(src2 == hv + first + _seg((2 * hv, vh * hd), 1, hd), 1.0, 0.0).astype(BF16)
        gcol128 = _dot_sel_right(gb, sel128)
        grp.append(dict(q=l2(q_ref[:, g * 2 * hd:(g + 1) * 2 * hd]) * (hd ** -0.5),
                        k=l2(k_ref[:, g * 2 * hd:(g + 1) * 2 * hd]),
                        v=v_ref[:, g * vh * hd:(g + 1) * vh * hd],
                        bcol=_dot_sel_right(gb, sel64(0)), gcol=_dot_sel_right(gb, sel64(hv)),
                        gcol128=gcol128, e_g=jnp.exp(gcol128)))

    res = {}

    def prepare(g, ci):
        d = grp[g]
        sl = slice(ci * c, (ci + 1) * c)
        qc, kc, vc = d["q"][sl], d["k"][sl], d["v"][sl]
        gcol128 = d["gcol128"][sl]
        grow = grow_ref[ci:ci + 1, g * vh * c:(g + 1) * vh * c]
        brow = brow_ref[ci:ci + 1, g * vh * c:(g + 1) * vh * c]
        decay = jnp.exp(jnp.where(incl, d["gcol"][sl] - grow, 0.0))
        both = _dot_nt(jnp.concatenate([kc, qc], axis=0), _block_diag(kc, kmask))
        yield
        amat = jnp.where(strict, both[:c] * decay * d["bcol"][sl], 0.0)
        attn = jnp.where(incl, both[c:] * decay, 0.0)
        inv = {}
        yield from _neumann_chain(-amat, eye_l, bd_cc, inv)
        tm = inv["tm"]
        k4 = rep(kc)
        u = _dot(tm * brow, bd(vc))
        wk = _dot(tm * (brow * jnp.exp(grow)), bd(k4))
        yield
        glast = gcol128[c - 1:c, :]
        ks = k4 * jnp.exp(glast - gcol128)
        zs = [jnp.where(pair_mask, _dot_tn(ks[:, ln], wk[:, ln]), 0.0) for ln in pairs]
        ns = [jnp.where(pair_mask, _dot_tn(ks[:, ln], u[:, ln]), 0.0) for ln in pairs]
        yield
        q4 = rep(qc)
        res[g, ci] = dict(lhs=[jnp.concatenate([wk[:, ln], q4[:, ln]], axis=0) for ln in pairs],
                          u=u, attn=attn, zs=zs, ns=ns, e_last=jnp.exp(glast), e_g=d["e_g"][sl])

    os_ = [[] for _ in range(n_groups)]

    def advance(g):
        s_pair = [s_ref[g, 0], s_ref[g, 1]]
        for ci in range(n_chunks):
            d = res[g, ci]
            ws = [_dot(d["lhs"][p], s_pair[p]) for p in range(2)]
            s_pair = [s_pair[p] * d["e_last"][:, pairs[p]] + d["ns"][p] - _dot(d["zs"][p], s_pair[p])
                      for p in range(2)]
            yield
            v_new = d["u"] - jnp.concatenate([w[:c] for w in ws], axis=1)
            os_[g].append(jnp.concatenate([w[c:] for w in ws], axis=1) * d["e_g"] + _dot(d["attn"], bd(v_new)))
            yield
        s_ref[g, 0] = s_pair[0]
        s_ref[g, 1] = s_pair[1]

    _lockstep(prepare(g, ci) for ci in range(n_chunks) for g in range(n_groups))
    _lockstep(advance(g) for g in range(n_groups))

    ones4 = _block_ones(vh * hd, hd)
    for g in range(n_groups):
        ln = slice(g * vh * hd, (g + 1) * vh * hd)
        o = jnp.concatenate(os_[g], axis=0)
        ms = _segsum(o * o, ones4) * (1.0 / hd)
        z = z_ref[:, ln]
        o_ref[:, ln] = (o * lax.rsqrt(ms + NORM_EPS) * nw_ref[0:1, :] * (z * jax.nn.sigmoid(z))).astype(o_ref.dtype)


def _gdn_chunk(qkv, zed, gates, g_rows, b_rows, norm_w4, batch, seq_len, key_dim, value_dim):
    m = qkv.shape[0]
    hv = gates.shape[1] // 2
    hd = GDN_HEAD
    tb = _pick(seq_len, (256, 128, 64))
    nt = seq_len // tb
    n_groups = 2 if value_dim % (8 * hd) == 0 else 1
    kw = n_groups * 2 * hd
    vw = n_groups * 4 * hd
    kb = key_dim // kw
    vb = 2 * key_dim // vw
    row = lambda b, j, t: b * nt + t
    rows_per_tb = tb // CHUNK
    g_rows = g_rows.reshape(m // tb, rows_per_tb, hv * CHUNK)
    b_rows = b_rows.reshape(m // tb, rows_per_tb, hv * CHUNK)
    rows_spec = pl.BlockSpec((None, rows_per_tb, n_groups * 4 * CHUNK), lambda b, j, t: (row(b, j, t), 0, j))
    return pl.pallas_call(
        functools.partial(_gdn_chunk_kernel, hv=hv, n_chunks=tb // CHUNK, n_groups=n_groups),
        grid=(batch, value_dim // vw, nt),
        in_specs=[pl.BlockSpec((tb, kw), lambda b, j, t: (row(b, j, t), j)),
                  pl.BlockSpec((tb, kw), lambda b, j, t: (row(b, j, t), kb + j)),
                  pl.BlockSpec((tb, vw), lambda b, j, t: (row(b, j, t), vb + j)),
                  pl.BlockSpec((tb, vw), lambda b, j, t: (row(b, j, t), j)),
                  pl.BlockSpec((tb, 2 * hv), lambda b, j, t: (row(b, j, t), 0)),
                  rows_spec, rows_spec,
                  pl.BlockSpec((V7X_SUBLANES, 4 * hd), lambda b, j, t: (0, 0))],
        out_specs=pl.BlockSpec((tb, vw), lambda b, j, t: (row(b, j, t), j)),
        out_shape=jax.ShapeDtypeStruct((m, value_dim), BF16),
        scratch_shapes=[pltpu.VMEM((n_groups, 2, 2 * hd, 2 * hd), F32)],
        compiler_params=_params("parallel", "parallel", "arbitrary"),
        name="gdn_chunk",
    )(qkv, qkv, qkv, zed, gates, g_rows, b_rows, norm_w4)


def _pad_rows(w, rows):
    return jnp.pad(w, ((0, rows - w.shape[0]), (0, 0)))


def _pad_cols(w, cols):
    return jnp.pad(w, ((0, 0), (0, cols - w.shape[1])))


def _rwkv_layer(x32, v_first, p, batch, seq_len, alpha, ln_g, ln_b):
    m, d = x32.shape
    lp = V7X_LANES
    has_vres = v_first is not None
    mu8 = _pad_rows(p["mu"], V7X_SUBLANES)
    v1 = p["v1"] if has_vres else jnp.zeros((d, lp), F32)
    wd = jnp.concatenate([_pad_cols(p["w1"], lp), _pad_cols(p["a1"], lp), _pad_cols(v1, lp), p["g1"]],
                         axis=1).astype(BF16)
    xm, lh = _rwkv_in(x32, mu8, wd, seq_len, has_vres)
    ws = jnp.stack([p["w_r"], p["w_k"], p["w_v"]]).astype(BF16)
    rkv = _matmul_stack(xm, ws, F32)
    v2 = p["v2"] if has_vres else jnp.zeros((lp, d), F32)
    wup = jnp.stack([_pad_rows(p["w2"], lp), _pad_rows(p["a2"], lp), _pad_rows(v2, lp)]).astype(BF16)
    v0 = p["v0"] if has_vres else jnp.zeros((d,), F32)
    vecs = jnp.stack([p["w0"], p["a0"], v0, p["k_k"], p["k_a"], p["r_k"].reshape(d), p["lnx_g"], p["lnx_b"]])
    vf = v_first if has_vres else rkv
    z = _rwkv_chunk(rkv, vf, lh, wup, p["g2"].astype(BF16), vecs, batch, seq_len, has_vres)
    x32, x16 = _matmul_res_ln(z, p["w_o"].astype(BF16), x32, ln_g, ln_b, alpha)
    return x32, x16, vf


def _gdn_layer(x32, x16, p, batch, seq_len, alpha, ln_g, ln_b):
    m, d = x32.shape
    hv = p["a_log"].shape[0]
    value_dim = p["w_out"].shape[0]
    conv_ch = p["conv_w"].shape[1]
    key_dim = (conv_ch - value_dim) // 2
    w_in = p["w_in"]
    taps = p["conv_w"].shape[0]
    qkv = _matmul_conv(x16, w_in[:, :conv_ch].astype(BF16), _conv_table(p["conv_w"], None, conv_ch),
                       seq_len, taps, F32)
    zed = _matmul_stack(x16[None], w_in[None, :, conv_ch:conv_ch + value_dim].astype(BF16), F32)[0]
    gates, gates_t = _gdn_gates(x16, w_in[:, conv_ch + value_dim:].astype(BF16), p["a_log"], p["dt_bias"])
    rows = lambda t: t.reshape(hv, m // CHUNK, CHUNK).transpose(1, 0, 2).reshape(m // CHUNK, hv * CHUNK)
    b_rows = rows(gates_t[:hv])
    g_rows = rows(gates_t[hv:])
    norm_w4 = jnp.tile(_pad_rows(p["norm_w"].reshape(1, GDN_HEAD), V7X_SUBLANES), (1, 4))
    o = _gdn_chunk(qkv, zed, gates, g_rows, b_rows, norm_w4, batch, seq_len, key_dim, value_dim)
    return _matmul_res_ln(o, p["w_out"].astype(BF16), x32, ln_g, ln_b, alpha)


def _ffn_layer(x32, x16, p, seq_len, alpha, ln_g, ln_b):
    d_ff = p["w_down"].shape[0]
    ffp = _round_up(d_ff, 512 if d_ff >= 2048 else V7X_LANES)
    w_up, conv_w, conv_b = p["w_up"], p["conv_w"], p["conv_b"]
    wg = _pad_cols(w_up[:, :d_ff], ffp).astype(BF16)
    wu = _pad_cols(w_up[:, d_ff:], ffp).astype(BF16)
    tab_g = _conv_table(conv_w[:, :d_ff], conv_b[:d_ff], ffp)
    tab_u = _conv_table(conv_w[:, d_ff:], conv_b[d_ff:], ffp)
    act = _matmul_conv(x16, wg, tab_g, seq_len, conv_w.shape[0], BF16, w_up=wu, tab_up=tab_u)
    w_down = _pad_rows(p["w_down"], ffp).astype(BF16)
    return _matmul_res_ln(act, w_down, x32, ln_g, ln_b, alpha)


def kernel(x, rw_mu, rw_w_r, rw_w_k, rw_w_v, rw_w_o, rw_w0, rw_w1, rw_w2, rw_a0, rw_a1, rw_a2, rw_v0, rw_v1, rw_v2, rw_g1, rw_g2, rw_k_k, rw_k_a, rw_r_k, rw_lnx_g, rw_lnx_b, gdn_w_in, gdn_conv_w, gdn_a_log, gdn_dt_bias, gdn_norm_w, gdn_w_out, ffn_w_up, ffn_conv_w, ffn_conv_b, ffn_w_down, ln_mix_g, ln_mix_b, ln_ffn_g, ln_ffn_b):
    batch, seq_len, d = x.shape
    depth = ln_mix_g.shape[0]
    alpha = (2 * depth) ** 0.25
    x32 = x.reshape(batch * seq_len, d)
    x16 = None
    v_first = None
    for i in range(depth):
        j = i // 2
        if i % 2 == 0:
            p = dict(mu=rw_mu[j], w_r=rw_w_r[j], w_k=rw_w_k[j], w_v=rw_w_v[j], w_o=rw_w_o[j], w0=rw_w0[j],
                     w1=rw_w1[j], w2=rw_w2[j], a0=rw_a0[j], a1=rw_a1[j], a2=rw_a2[j], g1=rw_g1[j], g2=rw_g2[j],
                     k_k=rw_k_k[j], k_a=rw_k_a[j], r_k=rw_r_k[j], lnx_g=rw_lnx_g[j], lnx_b=rw_lnx_b[j])
            if j > 0:
                p.update(v0=rw_v0[j - 1], v1=rw_v1[j - 1], v2=rw_v2[j - 1])
            x32, x16, v_first = _rwkv_layer(x32, v_first, p, batch, seq_len, alpha, ln_mix_g[i], ln_mix_b[i])
        else:
            p = dict(w_in=gdn_w_in[j], conv_w=gdn_conv_w[j], a_log=gdn_a_log[j], dt_bias=gdn_dt_bias[j],
                     norm_w=gdn_norm_w[j], w_out=gdn_w_out[j])
            x32, x16 = _gdn_layer(x32, x16, p, batch, seq_len, alpha, ln_mix_g[i], ln_mix_b[i])
        p = dict(w_up=ffn_w_up[i], conv_w=ffn_conv_w[i], conv_b=ffn_conv_b[i], w_down=ffn_w_down[i])
        x32, x16 = _ffn_layer(x32, x16, p, seq_len, alpha, ln_ffn_g[i], ln_ffn_b[i])
    return x32.reshape(batch, seq_len, d)
```

```python
import functools
import math

import jax
import jax.numpy as jnp
from jax import lax
from jax.experimental import pallas as pl
from jax.experimental.pallas import tpu as pltpu

F32 = jnp.float32
BF16 = jnp.bfloat16

RW_HEAD = 64
GDN_HEAD = 128
LN_EPS = 1e-5
NORM_EPS = 1e-6
RW_LNX_EPS = 64e-5

V7X_LANES = 128
V7X_SUBLANES = 8
V7X_MXU_DIM = 256
V7X_VMEM_BYTES = 64 * 1024 * 1024
VMEM_LIMIT = V7X_VMEM_BYTES * 7 // 8

CHUNK = 64
GROUP = V7X_MXU_DIM
HALO = V7X_SUBLANES


def _params(*sem):
    return pltpu.CompilerParams(dimension_semantics=sem, vmem_limit_bytes=VMEM_LIMIT)


def _pick(n, prefs):
    for p in prefs:
        if p <= n and n % p == 0:
            return p
    return n


def _round_up(n, m):
    return (n + m - 1) // m * m


def _dot(a, b):
    return jnp.dot(a.astype(BF16), b.astype(BF16), preferred_element_type=F32)


def _dot_nt(a, b):
    return lax.dot_general(a.astype(BF16), b.astype(BF16), (((1,), (1,)), ((), ())),
                           preferred_element_type=F32)


def _dot_tn(a, b):
    return lax.dot_general(a.astype(BF16), b.astype(BF16), (((0,), (0,)), ((), ())),
                           preferred_element_type=F32)


def _split(x, terms):
    out = []
    rest = x
    for _ in range(terms):
        t = rest.astype(BF16)
        out.append(t)
        rest = rest - t.astype(F32)
    return out


def _dot_sel_left(sel, x, terms=3):
    acc = None
    for t in _split(x, terms):
        p = jnp.dot(sel, t, preferred_element_type=F32)
        acc = p if acc is None else acc + p
    return acc


def _dot_sel_right(x, sel, terms=3):
    acc = None
    for t in _split(x, terms):
        p = jnp.dot(t, sel, preferred_element_type=F32)
        acc = p if acc is None else acc + p
    return acc


def _iota(shape, dim):
    return lax.broadcasted_iota(jnp.int32, shape, dim)


def _seg(shape, dim, seg):
    return lax.shift_right_logical(_iota(shape, dim), int(math.log2(seg)))


def _pos(shape, dim, seg):
    return lax.bitwise_and(_iota(shape, dim), seg - 1)


def _block_ones(n, seg):
    return jnp.where(_seg((n, n), 0, seg) == _seg((n, n), 1, seg), 1.0, 0.0).astype(BF16)


def _block_tri(n, seg, upper):
    r, c = _iota((n, n), 0), _iota((n, n), 1)
    same = _seg((n, n), 0, seg) == _seg((n, n), 1, seg)
    tri = (r <= c) if upper else (r >= c)
    return jnp.where(jnp.logical_and(same, tri), 1.0, 0.0).astype(BF16)


def _segsum(x, ones_blk):
    return _dot_sel_right(x, ones_blk, terms=1)


def _bd_mask(rows, lanes, row_seg, lane_seg):
    return _seg((rows, lanes), 0, row_seg) == _seg((rows, lanes), 1, lane_seg)


def _block_diag(x, mask):
    x16 = x.astype(BF16)
    tall = jnp.concatenate([x16] * (mask.shape[0] // x.shape[0]), axis=0)
    return jnp.where(mask, tall, jnp.zeros_like(tall))


def _layer_norm(y, g, b):
    mu = jnp.mean(y, axis=-1, keepdims=True)
    yc = y - mu
    var = jnp.mean(yc * yc, axis=-1, keepdims=True)
    return yc * lax.rsqrt(var + LN_EPS) * g + b


def _mm_kernel(x_ref, w_ref, o_ref):
    o_ref[...] = jnp.dot(x_ref[...], w_ref[...], preferred_element_type=F32).astype(o_ref.dtype)


def _matmul_stack(xs, ws, out_dtype):
    p, m, k = xs.shape
    n = ws.shape[2]
    bm = _pick(m, (1024, 512, 256, 128))
    bn = _pick(n, (1024, 512, 256, 128))
    return pl.pallas_call(
        _mm_kernel,
        grid=(p, n // bn, m // bm),
        in_specs=[pl.BlockSpec((None, bm, k), lambda q, j, i: (q, i, 0)),
                  pl.BlockSpec((None, k, bn), lambda q, j, i: (q, 0, j))],
        out_specs=pl.BlockSpec((None, bm, bn), lambda q, j, i: (q, i, j)),
        out_shape=jax.ShapeDtypeStruct((p, m, n), out_dtype),
        compiler_params=_params("parallel", "parallel", "parallel"),
        name="matmul_stack",
    )(xs, ws)


def _mm_res_ln_kernel(a_ref, w_ref, res_ref, gb_ref, o32_ref, o16_ref, *, alpha, nk, sub):
    kk = pl.program_id(1)
    blocks = [slice(s * sub, (s + 1) * sub) for s in range(a_ref.shape[0] // sub)]
    part = lambda rows: jnp.dot(a_ref[rows, :], w_ref[...], preferred_element_type=F32)

    @pl.when(kk == 0)
    def _():
        for rows in blocks:
            o32_ref[rows, :] = part(rows)

    @pl.when(kk > 0)
    def _():
        for rows in blocks:
            o32_ref[rows, :] += part(rows)

    @pl.when(kk == nk - 1)
    def _():
        for rows in blocks:
            y = alpha * res_ref[rows, :] + o32_ref[rows, :]
            out = _layer_norm(y, gb_ref[0:1, :], gb_ref[1:2, :])
            o32_ref[rows, :] = out
            o16_ref[rows, :] = out.astype(BF16)


def _matmul_res_ln(a, w, resid, gamma, beta, alpha):
    m, k = a.shape
    d = w.shape[1]
    bm = _pick(m, (1024, 512, 256, 128))
    bk = _pick(k, (512, 256, 128))
    nk = k // bk
    gb = jnp.concatenate([gamma.reshape(1, d), beta.reshape(1, d),
                          jnp.zeros((V7X_SUBLANES - 2, d), F32)], axis=0)
    return pl.pallas_call(
        functools.partial(_mm_res_ln_kernel, alpha=alpha, nk=nk, sub=_pick(bm, (512, 256, 128))),
        grid=(m // bm, nk),
        in_specs=[pl.BlockSpec((bm, bk), lambda i, kk: (i, kk)),
                  pl.BlockSpec((bk, d), lambda i, kk: (kk, 0)),
                  pl.BlockSpec((bm, d), lambda i, kk: (i, 0)),
                  pl.BlockSpec((V7X_SUBLANES, d), lambda i, kk: (0, 0))],
        out_specs=[pl.BlockSpec((bm, d), lambda i, kk: (i, 0)),
                   pl.BlockSpec((bm, d), lambda i, kk: (i, 0))],
        out_shape=[jax.ShapeDtypeStruct((m, d), F32), jax.ShapeDtypeStruct((m, d), BF16)],
        compiler_params=_params("parallel", "arbitrary"),
        name="matmul_res_ln",
    )(a, w, resid, gb)


def _shift_rows(h, prev, shift):
    rolled = pltpu.roll(h, shift=shift, axis=0)
    head = jnp.where(_iota(prev.shape, 0) < shift, pltpu.roll(prev, shift=shift, axis=0), rolled[0:HALO])
    return jnp.concatenate([head, rolled[HALO:]], axis=0)


def _mm_conv_kernel(x_ref, xh_ref, *refs, taps, glu, seq_len, bm):
    if glu:
        wg_ref, wu_ref, cg_ref, cu_ref, o_ref = refs
    else:
        wg_ref, cg_ref, o_ref = refs
    i = pl.program_id(1)
    valid = jnp.where((i * bm) % seq_len == 0, 0.0, 1.0).astype(F32)
    x = x_ref[...]
    xh = xh_ref[...]

    def branch(w_ref, c_ref):
        h = jnp.dot(x, w_ref[...], preferred_element_type=F32)
        hh = jnp.dot(xh, w_ref[...], preferred_element_type=F32) * valid
        c = c_ref[...]
        out = c[taps:taps + 1, :] + c[taps - 1:taps, :] * h
        for k in range(taps - 1):
            out = out + c[k:k + 1, :] * _shift_rows(h, hh, taps - 1 - k)
        return out

    half = 0.5 * branch(wg_ref, cg_ref)
    act = half + half * jnp.tanh(half)
    if glu:
        act = act * branch(wu_ref, cu_ref)
    o_ref[...] = act.astype(o_ref.dtype)


def _conv_table(conv_w, bias, n_pad):
    taps, n = conv_w.shape
    rows = [conv_w, (jnp.zeros((1, n), F32) if bias is None else bias.reshape(1, n)),
            jnp.zeros((V7X_SUBLANES - taps - 1, n), F32)]
    tab = jnp.concatenate(rows, axis=0)
    return jnp.pad(tab, ((0, 0), (0, n_pad - n)))


def _matmul_conv(x16, w, tab, seq_len, taps, out_dtype, w_up=None, tab_up=None):
    m, k = x16.shape
    n = w.shape[1]
    glu = w_up is not None
    bm = _pick(seq_len, (1024, 512, 256, 128, 64, 32, 16, 8))
    bn = _pick(n, (512, 256, 128))
    hb = bm // HALO
    x_spec = pl.BlockSpec((bm, k), lambda j, i: (i, 0))
    xh_spec = pl.BlockSpec((HALO, k), lambda j, i: (jnp.maximum(i * hb - 1, 0), 0))
    w_spec = pl.BlockSpec((k, bn), lambda j, i: (0, j))
    c_spec = pl.BlockSpec((V7X_SUBLANES, bn), lambda j, i: (0, j))
    if glu:
        in_specs = [x_spec, xh_spec, w_spec, w_spec, c_spec, c_spec]
        args = (x16, x16, w, w_up, tab, tab_up)
    else:
        in_specs = [x_spec, xh_spec, w_spec, c_spec]
        args = (x16, x16, w, tab)
    return pl.pallas_call(
        functools.partial(_mm_conv_kernel, taps=taps, glu=glu, seq_len=seq_len, bm=bm),
        grid=(n // bn, m // bm),
        in_specs=in_specs,
        out_specs=pl.BlockSpec((bm, bn), lambda j, i: (i, j)),
        out_shape=jax.ShapeDtypeStruct((m, n), out_dtype),
        compiler_params=_params("parallel", "parallel"),
        name="matmul_conv_glu" if glu else "matmul_conv",
    )(*args)


def _rwkv_in_kernel(x_ref, xh_ref, mu_ref, wd_ref, xm_ref, lh_ref, *, seq_len, bm, has_vres):
    i = pl.program_id(0)
    valid = jnp.where((i * bm) % seq_len == 0, 0.0, 1.0).astype(F32)
    x = x_ref[...]
    prev_row = xh_ref[HALO - 1:HALO, :] * valid
    rolled = pltpu.roll(x, shift=1, axis=0)
    x_prev = jnp.where(_iota(x.shape, 0) == 0, prev_row, rolled)
    xx = x_prev - x
    mu = mu_ref[...]
    mix = lambda r: (x + xx * mu[r:r + 1, :]).astype(BF16)
    xm_ref[0] = mix(0)
    xm_ref[1] = mix(2)
    xv = mix(3)
    xm_ref[2] = xv
    lp = V7X_LANES
    down = lambda xin, lo, hi: jnp.dot(xin, wd_ref[:, lo:hi], preferred_element_type=F32)
    lh_ref[:, 0:lp] = jnp.tanh(down(mix(1), 0, lp)).astype(BF16)
    lh_ref[:, lp:2 * lp] = down(mix(4), lp, 2 * lp).astype(BF16)
    if has_vres:
        lh_ref[:, 2 * lp:3 * lp] = down(xv, 2 * lp, 3 * lp).astype(BF16)
    else:
        lh_ref[:, 2 * lp:3 * lp] = jnp.zeros((bm, lp), BF16)
    lh_ref[:, 3 * lp:] = jax.nn.sigmoid(down(mix(5), 3 * lp, wd_ref.shape[1])).astype(BF16)


def _rwkv_in(x, mu8, wd, seq_len, has_vres):
    m, d = x.shape
    lh = wd.shape[1]
    bm = _pick(seq_len, (512, 256, 128, 64, 32, 16, 8))
    hb = bm // HALO
    return pl.pallas_call(
        functools.partial(_rwkv_in_kernel, seq_len=seq_len, bm=bm, has_vres=has_vres),
        grid=(m // bm,),
        in_specs=[pl.BlockSpec((bm, d), lambda i: (i, 0)),
                  pl.BlockSpec((HALO, d), lambda i: (jnp.maximum(i * hb - 1, 0), 0)),
                  pl.BlockSpec((V7X_SUBLANES, d), lambda i: (0, 0)),
                  pl.BlockSpec((d, lh), lambda i: (0, 0))],
        out_specs=[pl.BlockSpec((3, bm, d), lambda i: (0, i, 0)),
                   pl.BlockSpec((bm, lh), lambda i: (i, 0))],
        out_shape=[jax.ShapeDtypeStruct((3, m, d), BF16), jax.ShapeDtypeStruct((m, lh), BF16)],
        compiler_params=_params("parallel"),
        name="rwkv_in",
    )(x, x, mu8, wd)


def _lockstep(chains):
    chains = list(chains)
    while chains:
        alive = []
        for ch in chains:
            try:
                next(ch)
                alive.append(ch)
            except StopIteration:
                pass
        chains = alive


def _neumann_chain(nmat, eye_l, bd, out):
    c = nmat.shape[0]
    p = eye_l + nmat
    nk = _dot(nmat, bd(nmat))
    yield
    n = 2
    while 2 * n < c:
        both = _dot(jnp.concatenate([p, nk], axis=0), bd(nk))
        yield
        p = p + both[:c]
        nk = both[c:]
        n *= 2
    out["tm"] = p + _dot(p, bd(nk))
    yield


def _rwkv_chunk_kernel(r_ref, k_ref, v_ref, vf_ref, lh_ref, wup_ref, g2_ref, vec_ref, z_ref,
                       ht_ref, *, has_vres, n_chunks, n_groups):
    c = CHUNK
    hd = RW_HEAD
    lp = V7X_LANES

    @pl.when(pl.program_id(2) == 0)
    def _():
        ht_ref[...] = jnp.zeros(ht_ref.shape, F32)

    ones_blk = _block_ones(GROUP, hd)
    segsum = lambda t: _segsum(t, ones_blk)
    row = _iota((c, GROUP), 0)
    col = _pos((c, GROUP), 1, c)
    strict = row > col
    incl = row >= col
    eye_l = jnp.where(row == col, 1.0, 0.0).astype(F32)
    mask = _bd_mask(GROUP, GROUP, c, hd)
    bd = lambda t: _block_diag(t, mask)
    tri = _block_tri(n_chunks * c, c, upper=False)

    grp = []
    for g in range(n_groups):
        ln = slice(g * GROUP, (g + 1) * GROUP)
        vec = vec_ref[:, ln]
        w0, a0, v0, k_k, k_a, r_k, lnx_g, lnx_b = [vec[i:i + 1, :] for i in range(8)]
        r = r_ref[:, ln]
        k = k_ref[:, ln]
        v = v_ref[:, ln]
        up = lambda lo, hi, w: jnp.dot(lh_ref[:, lo:hi], w, preferred_element_type=F32)
        w = -jax.nn.softplus(-(w0 + up(0, lp, wup_ref[0, :, ln]))) - 0.5
        a = jax.nn.sigmoid(a0 + up(lp, 2 * lp, wup_ref[1, :, ln]))
        if has_vres:
            v = v + (vf_ref[:, ln] - v) * jax.nn.sigmoid(v0 + up(2 * lp, 3 * lp, wup_ref[2, :, ln]))
        gate = up(3 * lp, lh_ref.shape[1], g2_ref[:, ln])
        kk = k * k_k
        kk = kk * lax.rsqrt(segsum(kk * kk) + NORM_EPS)
        k2 = k * (1.0 + (a - 1.0) * k_a)
        bv = kk * a
        lw = -jnp.exp(w)
        cum = _dot_sel_left(tri, lw, terms=2)
        e_inv = jnp.exp(-cum)
        grp.append(dict(r=r, k2=k2, v=v, bv=bv, cum=cum, gate=gate, r_k=r_k, lnx_g=lnx_g, lnx_b=lnx_b,
                        at=-kk * jnp.exp(cum - lw), rt=r * jnp.exp(cum), bt=bv * e_inv, kt=k2 * e_inv))

    res = {}

    def prepare(g, ci):
        d = grp[g]
        sl = slice(ci * c, (ci + 1) * c)
        vc, at, rt = d["v"][sl], d["at"][sl], d["rt"][sl]
        last = d["cum"][(ci + 1) * c - 1:(ci + 1) * c, :]
        e_end = jnp.exp(last - d["cum"][sl])
        b_end = d["bv"][sl] * e_end
        k_end = d["k2"][sl] * e_end
        lhs = jnp.concatenate([at, rt], axis=0)
        xb = _dot_nt(lhs, bd(d["bt"][sl]))
        xk = _dot_nt(lhs, bd(d["kt"][sl]))
        yield
        aab = jnp.where(strict, xb[:c], 0.0)
        prb = jnp.where(incl, xb[c:], 0.0)
        aak = jnp.where(strict, xk[:c], 0.0)
        prk = jnp.where(incl, xk[c:], 0.0)
        xv = _dot(jnp.concatenate([aak, prk], axis=0), bd(vc))
        inv = {}
        yield from _neumann_chain(aab, eye_l, bd, inv)
        tm = inv["tm"]
        wmat = _dot(tm, bd(at))
        u0 = _dot(tm, bd(xv[:c]))
        yield
        q_bd = jnp.where(mask, _dot_tn(wmat, b_end), 0.0)
        n_bd = jnp.where(mask, _dot_tn(jnp.concatenate([u0, vc], axis=0),
                                       jnp.concatenate([b_end, k_end], axis=0)), 0.0)
        yield
        res[g, ci] = dict(lhs=jnp.concatenate([wmat, rt], axis=0), u0=u0, prb=prb, y0=xv[c:],
                          q_bd=q_bd, n_bd=n_bd, decay=jnp.exp(last))

    ys = [[] for _ in range(n_groups)]

    def advance(g):
        ht = ht_ref[g]
        for ci in range(n_chunks):
            d = res[g, ci]
            xh = _dot_nt(d["lhs"], ht)
            ht = d["decay"] * ht + _dot(ht, d["q_bd"]) + d["n_bd"]
            yield
            u = xh[:c] + d["u0"]
            ys[g].append(xh[c:] + _dot(d["prb"], bd(u)) + d["y0"])
            yield
        ht_ref[g] = ht

    _lockstep(prepare(g, ci) for ci in range(n_chunks) for g in range(n_groups))
    _lockstep(advance(g) for g in range(n_groups))

    inv_n = 1.0 / hd
    for g in range(n_groups):
        d = grp[g]
        y = jnp.concatenate(ys[g], axis=0)
        mean = segsum(y) * inv_n
        yc = y - mean
        var = segsum(yc * yc) * inv_n
        yn = yc * lax.rsqrt(var + RW_LNX_EPS) * d["lnx_g"] + d["lnx_b"]
        bonus = segsum(d["r"] * d["k2"] * d["r_k"]) * d["v"]
        z_ref[:, g * GROUP:(g + 1) * GROUP] = ((yn + bonus) * d["gate"]).astype(z_ref.dtype)


def _rwkv_chunk(rkv, v_first, lh, wup, g2, vecs, batch, seq_len, has_vres):
    _, m, d = rkv.shape
    lhw = lh.shape[1]
    tb = _pick(seq_len, (256, 128, 64))
    nt = seq_len // tb
    n_groups = _pick(d // GROUP, (4, 2, 1))
    lanes = n_groups * GROUP
    row = lambda b, j, t: b * nt + t
    plane = lambda p: pl.BlockSpec((None, tb, lanes), lambda b, j, t: (p, row(b, j, t), j))
    return pl.pallas_call(
        functools.partial(_rwkv_chunk_kernel, has_vres=has_vres, n_chunks=tb // CHUNK, n_groups=n_groups),
        grid=(batch, d // lanes, nt),
        in_specs=[plane(0), plane(1), plane(2), plane(2),
                  pl.BlockSpec((tb, lhw), lambda b, j, t: (row(b, j, t), 0)),
                  pl.BlockSpec((3, V7X_LANES, lanes), lambda b, j, t: (0, 0, j)),
                  pl.BlockSpec((g2.shape[0], lanes), lambda b, j, t: (0, j)),
                  pl.BlockSpec((V7X_SUBLANES, lanes), lambda b, j, t: (0, j))],
        out_specs=pl.BlockSpec((tb, lanes), lambda b, j, t: (row(b, j, t), j)),
        out_shape=jax.ShapeDtypeStruct((m, d), BF16),
        scratch_shapes=[pltpu.VMEM((n_groups, GROUP, GROUP), F32)],
        compiler_params=_params("parallel", "parallel", "arbitrary"),
        name="rwkv_chunk",
    )(rkv, rkv, rkv, v_first, lh, wup, g2, vecs)


def _gdn_gates_kernel(x_ref, w_ref, wt_ref, ad_ref, adt_ref, gb_ref, gt_ref, *, hv, bm):
    c = CHUNK
    x = x_ref[...]
    ba = jnp.dot(x, w_ref[...], preferred_element_type=F32)
    bat = lax.dot_general(wt_ref[...], x, (((1,), (1,)), ((), ())),
                          preferred_element_type=F32)
    ad = ad_ref[...]
    adt = adt_ref[...]
    g = -jnp.exp(ad[0:1, :]) * jax.nn.softplus(ba[:, hv:] + ad[1:2, :])
    gt = -jnp.exp(adt[:, 0:1]) * jax.nn.softplus(bat[hv:, :] + adt[:, 1:2])
    gb_ref[:, 0:hv] = jax.nn.sigmoid(ba[:, :hv])
    gt_ref[0:hv, :] = jax.nn.sigmoid(bat[:hv, :])
    gb_ref[:, hv:] = _dot_sel_left(_block_tri(bm, c, upper=False), g)
    gt_ref[hv:, :] = _dot_sel_right(gt, _block_tri(bm, c, upper=True))


def _gdn_gates(x16, w_ba, a_log, dt_bias):
    m, d = x16.shape
    hv = a_log.shape[0]
    bm = _pick(m, (512, 256, 128))
    ad = jnp.concatenate([a_log.reshape(1, hv), dt_bias.reshape(1, hv),
                          jnp.zeros((V7X_SUBLANES - 2, hv), F32)], axis=0)
    adt = jnp.transpose(ad)
    return pl.pallas_call(
        functools.partial(_gdn_gates_kernel, hv=hv, bm=bm),
        grid=(m // bm,),
        in_specs=[pl.BlockSpec((bm, d), lambda i: (i, 0)),
                  pl.BlockSpec((d, 2 * hv), lambda i: (0, 0)),
                  pl.BlockSpec((2 * hv, d), lambda i: (0, 0)),
                  pl.BlockSpec((V7X_SUBLANES, hv), lambda i: (0, 0)),
                  pl.BlockSpec((hv, V7X_SUBLANES), lambda i: (0, 0))],
        out_specs=[pl.BlockSpec((bm, 2 * hv), lambda i: (i, 0)),
                   pl.BlockSpec((2 * hv, bm), lambda i: (0, i))],
        out_shape=[jax.ShapeDtypeStruct((m, 2 * hv), F32), jax.ShapeDtypeStruct((2 * hv, m), F32)],
        compiler_params=_params("parallel"),
        name="gdn_gates",
    )(x16, w_ba, jnp.transpose(w_ba), ad, adt)


def _gdn_chunk_kernel(q_ref, k_ref, v_ref, z_ref, gb_ref, grow_ref, brow_ref, nw_ref, o_ref,
                      s_ref, *, hv, n_chunks, n_groups):
    c = CHUNK
    hd = GDN_HEAD
    vh = 4
    j = pl.program_id(1)

    @pl.when(pl.program_id(2) == 0)
    def _():
        s_ref[...] = jnp.zeros(s_ref.shape, F32)

    ones_blk = _block_ones(2 * hd, hd)
    segsum = lambda t: _segsum(t, ones_blk)
    l2 = lambda t: t * lax.rsqrt(segsum(t * t) + NORM_EPS)
    row = _iota((c, vh * c), 0)
    col = _pos((c, vh * c), 1, c)
    strict = row > col
    incl = row >= col
    eye_l = jnp.where(row == col, 1.0, 0.0).astype(F32)
    kmask = _bd_mask(vh * c, 2 * hd, 2 * c, hd)
    pair_mask = _bd_mask(2 * hd, 2 * hd, hd, hd)
    cc_mask = _bd_mask(vh * c, vh * c, c, c)
    wide_mask = _bd_mask(vh * c, vh * hd, c, hd)
    rep = lambda t: jnp.concatenate([t[:, :hd], t[:, :hd], t[:, hd:], t[:, hd:]], axis=1)
    bd = lambda t: _block_diag(t, wide_mask)
    bd_cc = lambda t: _block_diag(t, cc_mask)
    pairs = [slice(p * 2 * hd, (p + 1) * 2 * hd) for p in range(2)]

    gb = gb_ref[...]
    src = _iota((2 * hv, vh * c), 0)
    src2 = _iota((2 * hv, vh * hd), 0)
    grp = []
    for g in range(n_groups):
        first = vh * (n_groups * j + g)
        sel64 = lambda base: jnp.where(src == base + first + _seg((2 * hv, vh * c), 1, c),
                                       1.0, 0.0).astype(BF16)
        sel128 = jnp.where(src2 == hv + first + _seg((2 * hv, vh * hd), 1, hd), 1.0, 0.0).astype(BF16)
        gcol128 = _dot_sel_right(gb, sel128)
        grp.append(dict(q=l2(q_ref[:, g * 2 * hd:(g + 1) * 2 * hd]) * (hd ** -0.5),
                        k=l2(k_ref[:, g * 2 * hd:(g + 1) * 2 * hd]),
                        v=v_ref[:, g * vh * hd:(g + 1) * vh * hd],
                        bcol=_dot_sel_right(gb, sel64(0), terms=2), gcol=_dot_sel_right(gb, sel64(hv)),
                        gcol128=gcol128, e_g=jnp.exp(gcol128)))

    res = {}

    def prepare(g, ci):
        d = grp[g]
        sl = slice(ci * c, (ci + 1) * c)
        qc, kc, vc = d["q"][sl], d["k"][sl], d["v"][sl]
        gcol128 = d["gcol128"][sl]
        grow = grow_ref[ci:ci + 1, g * vh * c:(g + 1) * vh * c]
        brow = brow_ref[ci:ci + 1, g * vh * c:(g + 1) * vh * c]
        decay = jnp.exp(jnp.where(incl, d["gcol"][sl] - grow, 0.0))
        both = _dot_nt(jnp.concatenate([kc, qc], axis=0), _block_diag(kc, kmask))
        yield
        amat = jnp.where(strict, both[:c] * decay * d["bcol"][sl], 0.0)
        attn = jnp.where(incl, both[c:] * decay, 0.0)
        inv = {}
        yield from _neumann_chain(-amat, eye_l, bd_cc, inv)
        tm = inv["tm"]
        k4 = rep(kc)
        u = _dot(tm * brow, bd(vc))
        wk = _dot(tm * (brow * jnp.exp(grow)), bd(k4))
        yield
        glast = gcol128[c - 1:c, :]
        ks = k4 * jnp.exp(glast - gcol128)
        zs = [jnp.where(pair_mask, _dot_tn(ks[:, ln], wk[:, ln]), 0.0) for ln in pairs]
        ns = [jnp.where(pair_mask, _dot_tn(ks[:, ln], u[:, ln]), 0.0) for ln in pairs]
        yield
        q4 = rep(qc)
        res[g, ci] = dict(lhs=[jnp.concatenate([wk[:, ln], q4[:, ln]], axis=0) for ln in pairs],
                          u=u, attn=attn, zs=zs, ns=ns, e_last=jnp.exp(glast), e_g=d["e_g"][sl])

    os_ = [[] for _ in range(n_groups)]

    def advance(g):
        s_pair = [s_ref[g, 0], s_ref[g, 1]]
        for ci in range(n_chunks):
            d = res[g, ci]
            ws = [_dot(d["lhs"][p], s_pair[p]) for p in range(2)]
            s_pair = [s_pair[p] * d["e_last"][:, pairs[p]] + d["ns"][p] - _dot(d["zs"][p], s_pair[p])
                      for p in range(2)]
            yield
            v_new = d["u"] - jnp.concatenate([w[:c] for w in ws], axis=1)
            os_[g].append(jnp.concatenate([w[c:] for w in ws], axis=1) * d["e_g"] + _dot(d["attn"], bd(v_new)))
            yield
        s_ref[g, 0] = s_pair[0]
        s_ref[g, 1] = s_pair[1]

    _lockstep(prepare(g, ci) for ci in range(n_chunks) for g in range(n_groups))
    _lockstep(advance(g) for g in range(n_groups))

    ones4 = _block_ones(vh * hd, hd)
    for g in range(n_groups):
        ln = slice(g * vh * hd, (g + 1) * vh * hd)
        o = jnp.concatenate(os_[g], axis=0)
        ms = _segsum(o * o, ones4) * (1.0 / hd)
        z = z_ref[:, ln]
        o_ref[:, ln] = (o * lax.rsqrt(ms + NORM_EPS) * nw_ref[0:1, :] * (z * jax.nn.sigmoid(z))).astype(o_ref.dtype)


def _gdn_chunk(qkv, zed, gates, g_rows, b_rows, norm_w4, batch, seq_len, key_dim, value_dim):
    m = qkv.shape[0]
    hv = gates.shape[1] // 2
    hd = GDN_HEAD
    tb = _pick(seq_len, (256, 128, 64))
    nt = seq_len // tb
    n_groups = _pick(value_dim // (4 * hd), (4, 2, 1))
    kw = n_groups * 2 * hd
    vw = n_groups * 4 * hd
    kb = key_dim // kw
    vb = 2 * key_dim // vw
    row = lambda b, j, t: b * nt + t
    rows_per_tb = tb // CHUNK
    g_rows = g_rows.reshape(m // tb, rows_per_tb, hv * CHUNK)
    b_rows = b_rows.reshape(m // tb, rows_per_tb, hv * CHUNK)
    rows_spec = pl.BlockSpec((None, rows_per_tb, n_groups * 4 * CHUNK), lambda b, j, t: (row(b, j, t), 0, j))
    return pl.pallas_call(
        functools.partial(_gdn_chunk_kernel, hv=hv, n_chunks=tb // CHUNK, n_groups=n_groups),
        grid=(batch, value_dim // vw, nt),
        in_specs=[pl.BlockSpec((tb, kw), lambda b, j, t: (row(b, j, t), j)),
                  pl.BlockSpec((tb, kw), lambda b, j, t: (row(b, j, t), kb + j)),
                  pl.BlockSpec((tb, vw), lambda b, j, t: (row(b, j, t), vb + j)),
                  pl.BlockSpec((tb, vw), lambda b, j, t: (row(b, j, t), j)),
                  pl.BlockSpec((tb, 2 * hv), lambda b, j, t: (row(b, j, t), 0)),
                  rows_spec, rows_spec,
                  pl.BlockSpec((V7X_SUBLANES, 4 * hd), lambda b, j, t: (0, 0))],
        out_specs=pl.BlockSpec((tb, vw), lambda b, j, t: (row(b, j, t), j)),
        out_shape=jax.ShapeDtypeStruct((m, value_dim), BF16),
        scratch_shapes=[pltpu.VMEM((n_groups, 2, 2 * hd, 2 * hd), F32)],
        compiler_params=_params("parallel", "parallel", "arbitrary"),
        name="gdn_chunk",
    )(qkv, qkv, qkv, zed, gates, g_rows, b_rows, norm_w4)


def _pad_rows(w, rows):
    return jnp.pad(w, ((0, rows - w.shape[0]), (0, 0)))


def _pad_cols(w, cols):
    return jnp.pad(w, ((0, 0), (0, cols - w.shape[1])))


def _rwkv_layer(x32, v_first, p, batch, seq_len, alpha, ln_g, ln_b):
    m, d = x32.shape
    lp = V7X_LANES
    has_vres = v_first is not None
    mu8 = _pad_rows(p["mu"], V7X_SUBLANES)
    v1 = p["v1"] if has_vres else jnp.zeros((d, lp), F32)
    wd = jnp.concatenate([_pad_cols(p["w1"], lp), _pad_cols(p["a1"], lp), _pad_cols(v1, lp), p["g1"]],
                         axis=1).astype(BF16)
    xm, lh = _rwkv_in(x32, mu8, wd, seq_len, has_vres)
    ws = jnp.stack([p["w_r"], p["w_k"], p["w_v"]]).astype(BF16)
    rkv = _matmul_stack(xm, ws, F32)
    v2 = p["v2"] if has_vres else jnp.zeros((lp, d), F32)
    wup = jnp.stack([_pad_rows(p["w2"], lp), _pad_rows(p["a2"], lp), _pad_rows(v2, lp)]).astype(BF16)
    v0 = p["v0"] if has_vres else jnp.zeros((d,), F32)
    vecs = jnp.stack([p["w0"], p["a0"], v0, p["k_k"], p["k_a"], p["r_k"].reshape(d), p["lnx_g"], p["lnx_b"]])
    vf = v_first if has_vres else rkv
    z = _rwkv_chunk(rkv, vf, lh, wup, p["g2"].astype(BF16), vecs, batch, seq_len, has_vres)
    x32, x16 = _matmul_res_ln(z, p["w_o"].astype(BF16), x32, ln_g, ln_b, alpha)
    return x32, x16, vf


def _gdn_layer(x32, x16, p, batch, seq_len, alpha, ln_g, ln_b):
    m, d = x32.shape
    hv = p["a_log"].shape[0]
    value_dim = p["w_out"].shape[0]
    conv_ch = p["conv_w"].shape[1]
    key_dim = (conv_ch - value_dim) // 2
    w_in = p["w_in"]
    taps = p["conv_w"].shape[0]
    qkv = _matmul_conv(x16, w_in[:, :conv_ch].astype(BF16), _conv_table(p["conv_w"], None, conv_ch),
                       seq_len, taps, F32)
    zed = _matmul_stack(x16[None], w_in[None, :, conv_ch:conv_ch + value_dim].astype(BF16), F32)[0]
    gates, gates_t = _gdn_gates(x16, w_in[:, conv_ch + value_dim:].astype(BF16), p["a_log"], p["dt_bias"])
    rows = lambda t: t.reshape(hv, m // CHUNK, CHUNK).transpose(1, 0, 2).reshape(m // CHUNK, hv * CHUNK)
    b_rows = rows(gates_t[:hv])
    g_rows = rows(gates_t[hv:])
    norm_w4 = jnp.tile(_pad_rows(p["norm_w"].reshape(1, GDN_HEAD), V7X_SUBLANES), (1, 4))
    o = _gdn_chunk(qkv, zed, gates, g_rows, b_rows, norm_w4, batch, seq_len, key_dim, value_dim)
    return _matmul_res_ln(o, p["w_out"].astype(BF16), x32, ln_g, ln_b, alpha)


def _ffn_layer(x32, x16, p, seq_len, alpha, ln_g, ln_b):
    d_ff = p["w_down"].shape[0]
    ffp = _round_up(d_ff, 512 if d_ff >= 2048 else V7X_LANES)
    w_up, conv_w, conv_b = p["w_up"], p["conv_w"], p["conv_b"]
    wg = _pad_cols(w_up[:, :d_ff], ffp).astype(BF16)
    wu = _pad_cols(w_up[:, d_ff:], ffp).astype(BF16)
    tab_g = _conv_table(conv_w[:, :d_ff], conv_b[:d_ff], ffp)
    tab_u = _conv_table(conv_w[:, d_ff:], conv_b[d_ff:], ffp)
    act = _matmul_conv(x16, wg, tab_g, seq_len, conv_w.shape[0], BF16, w_up=wu, tab_up=tab_u)
    w_down = _pad_rows(p["w_down"], ffp).astype(BF16)
    return _matmul_res_ln(act, w_down, x32, ln_g, ln_b, alpha)


def kernel(x, rw_mu, rw_w_r, rw_w_k, rw_w_v, rw_w_o, rw_w0, rw_w1, rw_w2, rw_a0, rw_a1, rw_a2, rw_v0, rw_v1, rw_v2, rw_g1, rw_g2, rw_k_k, rw_k_a, rw_r_k, rw_lnx_g, rw_lnx_b, gdn_w_in, gdn_conv_w, gdn_a_log, gdn_dt_bias, gdn_norm_w, gdn_w_out, ffn_w_up, ffn_conv_w, ffn_conv_b, ffn_w_down, ln_mix_g, ln_mix_b, ln_ffn_g, ln_ffn_b):
    batch, seq_len, d = x.shape
    depth = ln_mix_g.shape[0]
    alpha = (2 * depth) ** 0.25
    x32 = x.reshape(batch * seq_len, d)
    x16 = None
    v_first = None
    for i in range(depth):
        j = i // 2
        if i % 2 == 0:
            p = dict(mu=rw_mu[j], w_r=rw_w_r[j], w_k=rw_w_k[j], w_v=rw_w_v[j], w_o=rw_w_o[j], w0=rw_w0[j],
                     w1=rw_w1[j], w2=rw_w2[j], a0=rw_a0[j], a1=rw_a1[j], a2=rw_a2[j], g1=rw_g1[j], g2=rw_g2[j],
                     k_k=rw_k_k[j], k_a=rw_k_a[j], r_k=rw_r_k[j], lnx_g=rw_lnx_g[j], lnx_b=rw_lnx_b[j])
            if j > 0:
                p.update(v0=rw_v0[j - 1], v1=rw_v1[j - 1], v2=rw_v2[j - 1])
            x32, x16, v_first = _rwkv_layer(x32, v_first, p, batch, seq_len, alpha, ln_mix_g[i], ln_mix_b[i])
        else:
            p = dict(w_in=gdn_w_in[j], conv_w=gdn_conv_w[j], a_log=gdn_a_log[j], dt_bias=gdn_dt_bias[j],
                     norm_w=gdn_norm_w[j], w_out=gdn_w_out[j])
            x32, x16 = _gdn_layer(x32, x16, p, batch, seq_len, alpha, ln_mix_g[i], ln_mix_b[i])
        p = dict(w_up=ffn_w_up[i], conv_w=ffn_conv_w[i], conv_b=ffn_conv_b[i], w_down=ffn_w_down[i])
        x32, x16 = _ffn_layer(x32, x16, p, seq_len, alpha, ln_ffn_g[i], ln_ffn_b[i])
    return x32.reshape(batch, seq_len, d)
```

```python
import functools
import math

import jax
import jax.numpy as jnp
from jax import lax
from jax.experimental import pallas as pl
from jax.experimental.pallas import tpu as pltpu

F32 = jnp.float32
BF16 = jnp.bfloat16

RW_HEAD = 64
GDN_HEAD = 128
LN_EPS = 1e-5
NORM_EPS = 1e-6
RW_LNX_EPS = 64e-5

V7X_LANES = 128
V7X_SUBLANES = 8
V7X_MXU_DIM = 256
V7X_VMEM_BYTES = 64 * 1024 * 1024
VMEM_LIMIT = V7X_VMEM_BYTES * 7 // 8

CHUNK = 64
GROUP = V7X_MXU_DIM
HALO = V7X_SUBLANES


def _params(*sem):
    return pltpu.CompilerParams(dimension_semantics=sem, vmem_limit_bytes=VMEM_LIMIT)


def _pick(n, prefs):
    for p in prefs:
        if p <= n and n % p == 0:
            return p
    return n


def _round_up(n, m):
    return (n + m - 1) // m * m


def _dot(a, b):
    return jnp.dot(a.astype(BF16), b.astype(BF16), preferred_element_type=F32)


def _dot_nt(a, b):
    return lax.dot_general(a.astype(BF16), b.astype(BF16), (((1,), (1,)), ((), ())),
                           preferred_element_type=F32)


def _dot_tn(a, b):
    return lax.dot_general(a.astype(BF16), b.astype(BF16), (((0,), (0,)), ((), ())),
                           preferred_element_type=F32)


def _split(x, terms):
    out = []
    rest = x
    for _ in range(terms):
        t = rest.astype(BF16)
        out.append(t)
        rest = rest - t.astype(F32)
    return out


def _dot_sel_left(sel, x, terms=3):
    acc = None
    for t in _split(x, terms):
        p = jnp.dot(sel, t, preferred_element_type=F32)
        acc = p if acc is None else acc + p
    return acc


def _dot_sel_right(x, sel, terms=3):
    acc = None
    for t in _split(x, terms):
        p = jnp.dot(t, sel, preferred_element_type=F32)
        acc = p if acc is None else acc + p
    return acc


def _iota(shape, dim):
    return lax.broadcasted_iota(jnp.int32, shape, dim)


def _seg(shape, dim, seg):
    return lax.shift_right_logical(_iota(shape, dim), int(math.log2(seg)))


def _pos(shape, dim, seg):
    return lax.bitwise_and(_iota(shape, dim), seg - 1)


def _block_ones(n, seg):
    return jnp.where(_seg((n, n), 0, seg) == _seg((n, n), 1, seg), 1.0, 0.0).astype(BF16)


def _block_tri(n, seg, upper):
    r, c = _iota((n, n), 0), _iota((n, n), 1)
    same = _seg((n, n), 0, seg) == _seg((n, n), 1, seg)
    tri = (r <= c) if upper else (r >= c)
    return jnp.where(jnp.logical_and(same, tri), 1.0, 0.0).astype(BF16)


def _segsum(x, ones_blk):
    return _dot_sel_right(x, ones_blk, terms=1)


def _bd_mask(rows, lanes, row_seg, lane_seg):
    return _seg((rows, lanes), 0, row_seg) == _seg((rows, lanes), 1, lane_seg)


def _block_diag(x, mask):
    x16 = x.astype(BF16)
    tall = jnp.concatenate([x16] * (mask.shape[0] // x.shape[0]), axis=0)
    return jnp.where(mask, tall, jnp.zeros_like(tall))


def _sigmoid(x):
    return 0.5 + 0.5 * jnp.tanh(0.5 * x)


def _layer_norm(y, g, b):
    mu = jnp.mean(y, axis=-1, keepdims=True)
    yc = y - mu
    var = jnp.mean(yc * yc, axis=-1, keepdims=True)
    return yc * lax.rsqrt(var + LN_EPS) * g + b


def _mm_kernel(x_ref, w_ref, o_ref):
    o_ref[...] = jnp.dot(x_ref[...], w_ref[...], preferred_element_type=F32).astype(o_ref.dtype)


def _matmul_stack(xs, ws, out_dtype):
    p, m, k = xs.shape
    n = ws.shape[2]
    bm = _pick(m, (1024, 512, 256, 128))
    bn = _pick(n, (1024, 512, 256, 128))
    return pl.pallas_call(
        _mm_kernel,
        grid=(p, n // bn, m // bm),
        in_specs=[pl.BlockSpec((None, bm, k), lambda q, j, i: (q, i, 0)),
                  pl.BlockSpec((None, k, bn), lambda q, j, i: (q, 0, j))],
        out_specs=pl.BlockSpec((None, bm, bn), lambda q, j, i: (q, i, j)),
        out_shape=jax.ShapeDtypeStruct((p, m, n), out_dtype),
        compiler_params=_params("parallel", "parallel", "parallel"),
        name="matmul_stack",
    )(xs, ws)


def _mm_res_ln_kernel(a_ref, w_ref, res_ref, gb_ref, o32_ref, o16_ref, *, alpha, nk, sub):
    kk = pl.program_id(1)
    blocks = [slice(s * sub, (s + 1) * sub) for s in range(a_ref.shape[0] // sub)]
    part = lambda rows: jnp.dot(a_ref[rows, :], w_ref[...], preferred_element_type=F32)

    @pl.when(kk == 0)
    def _():
        for rows in blocks:
            o32_ref[rows, :] = part(rows)

    @pl.when(kk > 0)
    def _():
        for rows in blocks:
            o32_ref[rows, :] += part(rows)

    @pl.when(kk == nk - 1)
    def _():
        for rows in blocks:
            y = alpha * res_ref[rows, :] + o32_ref[rows, :]
            out = _layer_norm(y, gb_ref[0:1, :], gb_ref[1:2, :])
            o32_ref[rows, :] = out
            o16_ref[rows, :] = out.astype(BF16)


def _matmul_res_ln(a, w, resid, gamma, beta, alpha):
    m, k = a.shape
    d = w.shape[1]
    bm = _pick(m, (1024, 512, 256, 128))
    bk = _pick(k, (512, 256, 128))
    nk = k // bk
    gb = jnp.concatenate([gamma.reshape(1, d), beta.reshape(1, d),
                          jnp.zeros((V7X_SUBLANES - 2, d), F32)], axis=0)
    return pl.pallas_call(
        functools.partial(_mm_res_ln_kernel, alpha=alpha, nk=nk, sub=_pick(bm, (512, 256, 128))),
        grid=(m // bm, nk),
        in_specs=[pl.BlockSpec((bm, bk), lambda i, kk: (i, kk)),
                  pl.BlockSpec((bk, d), lambda i, kk: (kk, 0)),
                  pl.BlockSpec((bm, d), lambda i, kk: (i, 0)),
                  pl.BlockSpec((V7X_SUBLANES, d), lambda i, kk: (0, 0))],
        out_specs=[pl.BlockSpec((bm, d), lambda i, kk: (i, 0)),
                   pl.BlockSpec((bm, d), lambda i, kk: (i, 0))],
        out_shape=[jax.ShapeDtypeStruct((m, d), F32), jax.ShapeDtypeStruct((m, d), BF16)],
        compiler_params=_params("parallel", "arbitrary"),
        name="matmul_res_ln",
    )(a, w, resid, gb)


def _shift_rows(h, prev, shift):
    rolled = pltpu.roll(h, shift=shift, axis=0)
    head = jnp.where(_iota(prev.shape, 0) < shift, pltpu.roll(prev, shift=shift, axis=0), rolled[0:HALO])
    return jnp.concatenate([head, rolled[HALO:]], axis=0)


def _mm_conv_kernel(x_ref, xh_ref, *refs, taps, glu, seq_len, bm):
    if glu:
        wg_ref, wu_ref, cg_ref, cu_ref, o_ref = refs
    else:
        wg_ref, cg_ref, o_ref = refs
    i = pl.program_id(1)
    valid = jnp.where((i * bm) % seq_len == 0, 0.0, 1.0).astype(F32)
    x = x_ref[...]
    xh = xh_ref[...]

    def branch(w_ref, c_ref):
        h = jnp.dot(x, w_ref[...], preferred_element_type=F32)
        hh = jnp.dot(xh, w_ref[...], preferred_element_type=F32) * valid
        c = c_ref[...]
        out = c[taps:taps + 1, :] + c[taps - 1:taps, :] * h
        for k in range(taps - 1):
            out = out + c[k:k + 1, :] * _shift_rows(h, hh, taps - 1 - k)
        return out

    half = 0.5 * branch(wg_ref, cg_ref)
    act = half + half * jnp.tanh(half)
    if glu:
        act = act * branch(wu_ref, cu_ref)
    o_ref[...] = act.astype(o_ref.dtype)


def _conv_table(conv_w, bias, n_pad):
    taps, n = conv_w.shape
    rows = [conv_w, (jnp.zeros((1, n), F32) if bias is None else bias.reshape(1, n)),
            jnp.zeros((V7X_SUBLANES - taps - 1, n), F32)]
    tab = jnp.concatenate(rows, axis=0)
    return jnp.pad(tab, ((0, 0), (0, n_pad - n)))


def _matmul_conv(x16, w, tab, seq_len, taps, out_dtype, w_up=None, tab_up=None):
    m, k = x16.shape
    n = w.shape[1]
    glu = w_up is not None
    bm = _pick(seq_len, (1024, 512, 256, 128, 64, 32, 16, 8))
    bn = _pick(n, (512, 256, 128))
    hb = bm // HALO
    x_spec = pl.BlockSpec((bm, k), lambda j, i: (i, 0))
    xh_spec = pl.BlockSpec((HALO, k), lambda j, i: (jnp.maximum(i * hb - 1, 0), 0))
    w_spec = pl.BlockSpec((k, bn), lambda j, i: (0, j))
    c_spec = pl.BlockSpec((V7X_SUBLANES, bn), lambda j, i: (0, j))
    if glu:
        in_specs = [x_spec, xh_spec, w_spec, w_spec, c_spec, c_spec]
        args = (x16, x16, w, w_up, tab, tab_up)
    else:
        in_specs = [x_spec, xh_spec, w_spec, c_spec]
        args = (x16, x16, w, tab)
    return pl.pallas_call(
        functools.partial(_mm_conv_kernel, taps=taps, glu=glu, seq_len=seq_len, bm=bm),
        grid=(n // bn, m // bm),
        in_specs=in_specs,
        out_specs=pl.BlockSpec((bm, bn), lambda j, i: (i, j)),
        out_shape=jax.ShapeDtypeStruct((m, n), out_dtype),
        compiler_params=_params("parallel", "parallel"),
        name="matmul_conv_glu" if glu else "matmul_conv",
    )(*args)


def _rwkv_in_kernel(x_ref, xh_ref, mu_ref, wd_ref, xm_ref, lh_ref, *, seq_len, bm, has_vres):
    i = pl.program_id(0)
    valid = jnp.where((i * bm) % seq_len == 0, 0.0, 1.0).astype(F32)
    x = x_ref[...]
    prev_row = xh_ref[HALO - 1:HALO, :] * valid
    rolled = pltpu.roll(x, shift=1, axis=0)
    x_prev = jnp.where(_iota(x.shape, 0) == 0, prev_row, rolled)
    xx = x_prev - x
    mu = mu_ref[...]
    mix = lambda r: (x + xx * mu[r:r + 1, :]).astype(BF16)
    xm_ref[0] = mix(0)
    xm_ref[1] = mix(2)
    xv = mix(3)
    xm_ref[2] = xv
    lp = V7X_LANES
    down = lambda xin, lo, hi: jnp.dot(xin, wd_ref[:, lo:hi], preferred_element_type=F32)
    lh_ref[:, 0:lp] = jnp.tanh(down(mix(1), 0, lp)).astype(BF16)
    lh_ref[:, lp:2 * lp] = down(mix(4), lp, 2 * lp).astype(BF16)
    if has_vres:
        lh_ref[:, 2 * lp:3 * lp] = down(xv, 2 * lp, 3 * lp).astype(BF16)
    else:
        lh_ref[:, 2 * lp:3 * lp] = jnp.zeros((bm, lp), BF16)
    lh_ref[:, 3 * lp:] = jax.nn.sigmoid(down(mix(5), 3 * lp, wd_ref.shape[1])).astype(BF16)


def _rwkv_in(x, mu8, wd, seq_len, has_vres):
    m, d = x.shape
    lh = wd.shape[1]
    bm = _pick(seq_len, (512, 256, 128, 64, 32, 16, 8))
    hb = bm // HALO
    return pl.pallas_call(
        functools.partial(_rwkv_in_kernel, seq_len=seq_len, bm=bm, has_vres=has_vres),
        grid=(m // bm,),
        in_specs=[pl.BlockSpec((bm, d), lambda i: (i, 0)),
                  pl.BlockSpec((HALO, d), lambda i: (jnp.maximum(i * hb - 1, 0), 0)),
                  pl.BlockSpec((V7X_SUBLANES, d), lambda i: (0, 0)),
                  pl.BlockSpec((d, lh), lambda i: (0, 0))],
        out_specs=[pl.BlockSpec((3, bm, d), lambda i: (0, i, 0)),
                   pl.BlockSpec((bm, lh), lambda i: (i, 0))],
        out_shape=[jax.ShapeDtypeStruct((3, m, d), BF16), jax.ShapeDtypeStruct((m, lh), BF16)],
        compiler_params=_params("parallel"),
        name="rwkv_in",
    )(x, x, mu8, wd)


def _lockstep(chains):
    chains = list(chains)
    while chains:
        alive = []
        for ch in chains:
            try:
                next(ch)
                alive.append(ch)
            except StopIteration:
                pass
        chains = alive


def _neumann_chain(nmat, eye_l, bd, out):
    c = nmat.shape[0]
    p = eye_l + nmat
    nk = _dot(nmat, bd(nmat))
    yield
    n = 2
    while 2 * n < c:
        both = _dot(jnp.concatenate([p, nk], axis=0), bd(nk))
        yield
        p = p + both[:c]
        nk = both[c:]
        n *= 2
    out["tm"] = p + _dot(p, bd(nk))
    yield


def _rwkv_chunk_kernel(r_ref, k_ref, v_ref, vf_ref, lh_ref, wup_ref, g2_ref, vec_ref, z_ref,
                       ht_ref, *, has_vres, n_chunks, n_groups):
    c = CHUNK
    hd = RW_HEAD
    lp = V7X_LANES

    @pl.when(pl.program_id(2) == 0)
    def _():
        ht_ref[...] = jnp.zeros(ht_ref.shape, F32)

    ones_blk = _block_ones(GROUP, hd)
    segsum = lambda t: _segsum(t, ones_blk)
    row = _iota((c, GROUP), 0)
    col = _pos((c, GROUP), 1, c)
    strict = row > col
    incl = row >= col
    eye_l = jnp.where(row == col, 1.0, 0.0).astype(F32)
    mask = _bd_mask(GROUP, GROUP, c, hd)
    bd = lambda t: _block_diag(t, mask)
    tri = _block_tri(n_chunks * c, c, upper=False)

    grp = []
    for g in range(n_groups):
        ln = slice(g * GROUP, (g + 1) * GROUP)
        vec = vec_ref[:, ln]
        w0, a0, v0, k_k, k_a, r_k, lnx_g, lnx_b = [vec[i:i + 1, :] for i in range(8)]
        r = r_ref[:, ln]
        k = k_ref[:, ln]
        v = v_ref[:, ln]
        up = lambda lo, hi, w: jnp.dot(lh_ref[:, lo:hi], w, preferred_element_type=F32)
        lw = -math.exp(-0.5) * _sigmoid(w0 + up(0, lp, wup_ref[0, :, ln]))
        a = _sigmoid(a0 + up(lp, 2 * lp, wup_ref[1, :, ln]))
        if has_vres:
            v = v + (vf_ref[:, ln] - v) * _sigmoid(v0 + up(2 * lp, 3 * lp, wup_ref[2, :, ln]))
        gate = up(3 * lp, lh_ref.shape[1], g2_ref[:, ln])
        kk = k * k_k
        kk = kk * lax.rsqrt(segsum(kk * kk) + NORM_EPS)
        k2 = k * (1.0 + (a - 1.0) * k_a)
        bv = kk * a
        cum = _dot_sel_left(tri, lw, terms=2)
        e_inv = jnp.exp(-cum)
        grp.append(dict(r=r, k2=k2, v=v, bv=bv, cum=cum, gate=gate, r_k=r_k, lnx_g=lnx_g, lnx_b=lnx_b,
                        e_inv=e_inv, at=-kk * jnp.exp(cum - lw), rt=r * jnp.exp(cum), bt=bv * e_inv,
                        kt=k2 * e_inv))

    res = {}

    def prepare(g, ci):
        d = grp[g]
        sl = slice(ci * c, (ci + 1) * c)
        vc, at, rt = d["v"][sl], d["at"][sl], d["rt"][sl]
        decay = jnp.exp(d["cum"][(ci + 1) * c - 1:(ci + 1) * c, :])
        e_end = decay * d["e_inv"][sl]
        bk_end = jnp.concatenate([d["bv"][sl] * e_end, d["k2"][sl] * e_end], axis=0)
        lhs = jnp.concatenate([at, rt], axis=0)
        xb = _dot_nt(lhs, bd(d["bt"][sl]))
        xk = _dot_nt(lhs, bd(d["kt"][sl]))
        yield
        aab = jnp.where(strict, xb[:c], 0.0)
        prb = jnp.where(incl, xb[c:], 0.0)
        aak = jnp.where(strict, xk[:c], 0.0)
        prk = jnp.where(incl, xk[c:], 0.0)
        xv = _dot(jnp.concatenate([aak, prk], axis=0), bd(vc))
        inv = {}
        yield from _neumann_chain(aab, eye_l, bd, inv)
        tm = inv["tm"]
        wmat = _dot(tm, bd(at))
        u0 = _dot(tm, bd(xv[:c]))
        yield
        res[g, ci] = dict(lhs=jnp.concatenate([wmat, rt], axis=0), u0=u0, prb=prb, y0=xv[c:], vc=vc,
                          bk_end=bk_end, decay=decay)

    ys = [[] for _ in range(n_groups)]

    def advance(g):
        ht = ht_ref[g]
        for ci in range(n_chunks):
            d = res[g, ci]
            xh = _dot_nt(d["lhs"], ht)
            yield
            u = xh[:c] + d["u0"]
            upd = _dot_tn(jnp.concatenate([u, d["vc"]], axis=0), d["bk_end"])
            ys[g].append(xh[c:] + _dot(d["prb"], bd(u)) + d["y0"])
            yield
            ht = d["decay"] * ht + jnp.where(mask, upd, 0.0)
        ht_ref[g] = ht

    _lockstep(prepare(g, ci) for ci in range(n_chunks) for g in range(n_groups))
    _lockstep(advance(g) for g in range(n_groups))

    inv_n = 1.0 / hd
    for g in range(n_groups):
        d = grp[g]
        y = jnp.concatenate(ys[g], axis=0)
        mean = segsum(y) * inv_n
        yc = y - mean
        var = segsum(yc * yc) * inv_n
        yn = yc * lax.rsqrt(var + RW_LNX_EPS) * d["lnx_g"] + d["lnx_b"]
        bonus = segsum(d["r"] * d["k2"] * d["r_k"]) * d["v"]
        z_ref[:, g * GROUP:(g + 1) * GROUP] = ((yn + bonus) * d["gate"]).astype(z_ref.dtype)


def _rwkv_chunk(rkv, v_first, lh, wup, g2, vecs, batch, seq_len, has_vres):
    _, m, d = rkv.shape
    lhw = lh.shape[1]
    tb = _pick(seq_len, (128, 64))
    nt = seq_len // tb
    n_groups = _pick(d // GROUP, (8, 4, 2, 1))
    lanes = n_groups * GROUP
    row = lambda b, j, t: b * nt + t
    plane = lambda p: pl.BlockSpec((None, tb, lanes), lambda b, j, t: (p, row(b, j, t), j))
    return pl.pallas_call(
        functools.partial(_rwkv_chunk_kernel, has_vres=has_vres, n_chunks=tb // CHUNK, n_groups=n_groups),
        grid=(batch, d // lanes, nt),
        in_specs=[plane(0), plane(1), plane(2), plane(2),
                  pl.BlockSpec((tb, lhw), lambda b, j, t: (row(b, j, t), 0)),
                  pl.BlockSpec((3, V7X_LANES, lanes), lambda b, j, t: (0, 0, j)),
                  pl.BlockSpec((g2.shape[0], lanes), lambda b, j, t: (0, j)),
                  pl.BlockSpec((V7X_SUBLANES, lanes), lambda b, j, t: (0, j))],
        out_specs=pl.BlockSpec((tb, lanes), lambda b, j, t: (row(b, j, t), j)),
        out_shape=jax.ShapeDtypeStruct((m, d), BF16),
        scratch_shapes=[pltpu.VMEM((n_groups, GROUP, GROUP), F32)],
        compiler_params=_params("parallel", "parallel", "arbitrary"),
        name="rwkv_chunk",
    )(rkv, rkv, rkv, v_first, lh, wup, g2, vecs)


def _gdn_gates_kernel(x_ref, w_ref, wt_ref, ad_ref, adt_ref, gb_ref, gt_ref, *, hv, bm):
    c = CHUNK
    x = x_ref[...]
    ba = jnp.dot(x, w_ref[...], preferred_element_type=F32)
    bat = lax.dot_general(wt_ref[...], x, (((1,), (1,)), ((), ())),
                          preferred_element_type=F32)
    ad = ad_ref[...]
    adt = adt_ref[...]
    g = -jnp.exp(ad[0:1, :]) * jax.nn.softplus(ba[:, hv:] + ad[1:2, :])
    gt = -jnp.exp(adt[:, 0:1]) * jax.nn.softplus(bat[hv:, :] + adt[:, 1:2])
    gb_ref[:, 0:hv] = jax.nn.sigmoid(ba[:, :hv])
    gt_ref[0:hv, :] = jax.nn.sigmoid(bat[:hv, :])
    gb_ref[:, hv:] = _dot_sel_left(_block_tri(bm, c, upper=False), g)
    gt_ref[hv:, :] = _dot_sel_right(gt, _block_tri(bm, c, upper=True))


def _gdn_gates(x16, w_ba, a_log, dt_bias):
    m, d = x16.shape
    hv = a_log.shape[0]
    bm = _pick(m, (512, 256, 128))
    ad = jnp.concatenate([a_log.reshape(1, hv), dt_bias.reshape(1, hv),
                          jnp.zeros((V7X_SUBLANES - 2, hv), F32)], axis=0)
    adt = jnp.transpose(ad)
    return pl.pallas_call(
        functools.partial(_gdn_gates_kernel, hv=hv, bm=bm),
        grid=(m // bm,),
        in_specs=[pl.BlockSpec((bm, d), lambda i: (i, 0)),
                  pl.BlockSpec((d, 2 * hv), lambda i: (0, 0)),
                  pl.BlockSpec((2 * hv, d), lambda i: (0, 0)),
                  pl.BlockSpec((V7X_SUBLANES, hv), lambda i: (0, 0)),
                  pl.BlockSpec((hv, V7X_SUBLANES), lambda i: (0, 0))],
        out_specs=[pl.BlockSpec((bm, 2 * hv), lambda i: (i, 0)),
                   pl.BlockSpec((2 * hv, bm), lambda i: (0, i))],
        out_shape=[jax.ShapeDtypeStruct((m, 2 * hv), F32), jax.ShapeDtypeStruct((2 * hv, m), F32)],
        compiler_params=_params("parallel"),
        name="gdn_gates",
    )(x16, w_ba, jnp.transpose(w_ba), ad, adt)


def _gdn_chunk_kernel(q_ref, k_ref, v_ref, z_ref, gb_ref, grow_ref, brow_ref, nw_ref, o_ref,
                      s_ref, *, hv, n_chunks, n_groups):
    c = CHUNK
    hd = GDN_HEAD
    vh = 4
    j = pl.program_id(1)

    @pl.when(pl.program_id(2) == 0)
    def _():
        s_ref[...] = jnp.zeros(s_ref.shape, F32)

    ones_blk = _block_ones(2 * hd, hd)
    segsum = lambda t: _segsum(t, ones_blk)
    l2 = lambda t: t * lax.rsqrt(segsum(t * t) + NORM_EPS)
    row = _iota((c, vh * c), 0)
    col = _pos((c, vh * c), 1, c)
    strict = row > col
    incl = row >= col
    eye_l = jnp.where(row == col, 1.0, 0.0).astype(F32)
    kmask = _bd_mask(vh * c, 2 * hd, 2 * c, hd)
    pair_mask = _bd_mask(2 * hd, 2 * hd, hd, hd)
    cc_mask = _bd_mask(vh * c, vh * c, c, c)
    wide_mask = _bd_mask(vh * c, vh * hd, c, hd)
    rep = lambda t: jnp.concatenate([t[:, :hd], t[:, :hd], t[:, hd:], t[:, hd:]], axis=1)
    bd = lambda t: _block_diag(t, wide_mask)
    bd_cc = lambda t: _block_diag(t, cc_mask)
    pairs = [slice(p * 2 * hd, (p + 1) * 2 * hd) for p in range(2)]

    gb = gb_ref[...]
    src = _iota((2 * hv, vh * c), 0)
    src2 = _iota((2 * hv, vh * hd), 0)
    grp = []
    for g in range(n_groups):
        first = vh * (n_groups * j + g)
        sel64 = lambda base: jnp.where(src == base + first + _seg((2 * hv, vh * c), 1, c),
                                       1.0, 0.0).astype(BF16)
        sel128 = jnp.where(src2 == hv + first + _seg((2 * hv, vh * hd), 1, hd), 1.0, 0.0).astype(BF16)
        gcol128 = _dot_sel_right(gb, sel128)
        grp.append(dict(q=l2(q_ref[:, g * 2 * hd:(g + 1) * 2 * hd]) * (hd ** -0.5),
                        k=l2(k_ref[:, g * 2 * hd:(g + 1) * 2 * hd]),
                        v=v_ref[:, g * vh * hd:(g + 1) * vh * hd],
                        bcol=_dot_sel_right(gb, sel64(0), terms=2), gcol=_dot_sel_right(gb, sel64(hv)),
                        gcol128=gcol128, e_g=jnp.exp(gcol128)))

    res = {}

    def prepare(g, ci):
        d = grp[g]
        sl = slice(ci * c, (ci + 1) * c)
        qc, kc, vc = d["q"][sl], d["k"][sl], d["v"][sl]
        gcol128 = d["gcol128"][sl]
        grow = grow_ref[ci:ci + 1, g * vh * c:(g + 1) * vh * c]
        brow = brow_ref[ci:ci + 1, g * vh * c:(g + 1) * vh * c]
        decay = jnp.exp(jnp.where(incl, d["gcol"][sl] - grow, 0.0))
        both = _dot_nt(jnp.concatenate([kc, qc], axis=0), _block_diag(kc, kmask))
        yield
        amat = jnp.where(strict, both[:c] * decay * d["bcol"][sl], 0.0)
        attn = jnp.where(incl, both[c:] * decay, 0.0)
        inv = {}
        yield from _neumann_chain(-amat, eye_l, bd_cc, inv)
        tm = inv["tm"]
        k4 = rep(kc)
        u = _dot(tm * brow, bd(vc))
        wk = _dot(tm * (brow * jnp.exp(grow)), bd(k4))
        yield
        glast = gcol128[c - 1:c, :]
        q4 = rep(qc)
        res[g, ci] = dict(lhs=[jnp.concatenate([wk[:, ln], q4[:, ln]], axis=0) for ln in pairs],
                          u=u, attn=attn, k4=k4, tail=jnp.exp(glast - gcol128), e_last=jnp.exp(glast),
                          e_g=d["e_g"][sl])

    os_ = [[] for _ in range(n_groups)]

    def advance(g):
        s_pair = [s_ref[g, 0], s_ref[g, 1]]
        for ci in range(n_chunks):
            d = res[g, ci]
            ws = [_dot(d["lhs"][p], s_pair[p]) for p in range(2)]
            yield
            v_new = d["u"] - jnp.concatenate([w[:c] for w in ws], axis=1)
            vs = v_new * d["tail"]
            upd = [_dot_tn(d["k4"][:, ln], vs[:, ln]) for ln in pairs]
            os_[g].append(jnp.concatenate([w[c:] for w in ws], axis=1) * d["e_g"] + _dot(d["attn"], bd(v_new)))
            yield
            s_pair = [s_pair[p] * d["e_last"][:, pairs[p]] + jnp.where(pair_mask, upd[p], 0.0)
                      for p in range(2)]
        s_ref[g, 0] = s_pair[0]
        s_ref[g, 1] = s_pair[1]

    _lockstep(prepare(g, ci) for ci in range(n_chunks) for g in range(n_groups))
    _lockstep(advance(g) for g in range(n_groups))

    ones4 = _block_ones(vh * hd, hd)
    for g in range(n_groups):
        ln = slice(g * vh * hd, (g + 1) * vh * hd)
        o = jnp.concatenate(os_[g], axis=0)
        ms = _segsum(o * o, ones4) * (1.0 / hd)
        z = z_ref[:, ln]
        o_ref[:, ln] = (o * lax.rsqrt(ms + NORM_EPS) * nw_ref[0:1, :] * (z * jax.nn.sigmoid(z))).astype(o_ref.dtype)


def _gdn_chunk(qkv, zed, gates, g_rows, b_rows, norm_w4, batch, seq_len, key_dim, value_dim):
    m = qkv.shape[0]
    hv = gates.shape[1] // 2
    hd = GDN_HEAD
    tb = _pick(seq_len, (128, 64))
    nt = seq_len // tb
    n_groups = _pick(value_dim // (4 * hd), (8, 4, 2, 1))
    kw = n_groups * 2 * hd
    vw = n_groups * 4 * hd
    kb = key_dim // kw
    vb = 2 * key_dim // vw
    row = lambda b, j, t: b * nt + t
    rows_per_tb = tb // CHUNK
    g_rows = g_rows.reshape(m // tb, rows_per_tb, hv * CHUNK)
    b_rows = b_rows.reshape(m // tb, rows_per_tb, hv * CHUNK)
    rows_spec = pl.BlockSpec((None, rows_per_tb, n_groups * 4 * CHUNK), lambda b, j, t: (row(b, j, t), 0, j))
    return pl.pallas_call(
        functools.partial(_gdn_chunk_kernel, hv=hv, n_chunks=tb // CHUNK, n_groups=n_groups),
        grid=(batch, value_dim // vw, nt),
        in_specs=[pl.BlockSpec((tb, kw), lambda b, j, t: (row(b, j, t), j)),
                  pl.BlockSpec((tb, kw), lambda b, j, t: (row(b, j, t), kb + j)),
                  pl.BlockSpec((tb, vw), lambda b, j, t: (row(b, j, t), vb + j)),
                  pl.BlockSpec((tb, vw), lambda b, j, t: (row(b, j, t), j)),
                  pl.BlockSpec((tb, 2 * hv), lambda b, j, t: (row(b, j, t), 0)),
                  rows_spec, rows_spec,
                  pl.BlockSpec((V7X_SUBLANES, 4 * hd), lambda b, j, t: (0, 0))],
        out_specs=pl.BlockSpec((tb, vw), lambda b, j, t: (row(b, j, t), j)),
        out_shape=jax.ShapeDtypeStruct((m, value_dim), BF16),
        scratch_shapes=[pltpu.VMEM((n_groups, 2, 2 * hd, 2 * hd), F32)],
        compiler_params=_params("parallel", "parallel", "arbitrary"),
        name="gdn_chunk",
    )(qkv, qkv, qkv, zed, gates, g_rows, b_rows, norm_w4)


def _pad_rows(w, rows):
    return jnp.pad(w, ((0, rows - w.shape[0]), (0, 0)))


def _pad_cols(w, cols):
    return jnp.pad(w, ((0, 0), (0, cols - w.shape[1])))


def _rwkv_layer(x32, v_first, p, batch, seq_len, alpha, ln_g, ln_b):
    m, d = x32.shape
    lp = V7X_LANES
    has_vres = v_first is not None
    mu8 = _pad_rows(p["mu"], V7X_SUBLANES)
    v1 = p["v1"] if has_vres else jnp.zeros((d, lp), F32)
    wd = jnp.concatenate([_pad_cols(p["w1"], lp), _pad_cols(p["a1"], lp), _pad_cols(v1, lp), p["g1"]],
                         axis=1).astype(BF16)
    xm, lh = _rwkv_in(x32, mu8, wd, seq_len, has_vres)
    ws = jnp.stack([p["w_r"], p["w_k"], p["w_v"]]).astype(BF16)
    rkv = _matmul_stack(xm, ws, F32)
    v2 = p["v2"] if has_vres else jnp.zeros((lp, d), F32)
    wup = jnp.stack([_pad_rows(p["w2"], lp), _pad_rows(p["a2"], lp), _pad_rows(v2, lp)]).astype(BF16)
    v0 = p["v0"] if has_vres else jnp.zeros((d,), F32)
    vecs = jnp.stack([p["w0"], p["a0"], v0, p["k_k"], p["k_a"], p["r_k"].reshape(d), p["lnx_g"], p["lnx_b"]])
    vf = v_first if has_vres else rkv
    z = _rwkv_chunk(rkv, vf, lh, wup, p["g2"].astype(BF16), vecs, batch, seq_len, has_vres)
    x32, x16 = _matmul_res_ln(z, p["w_o"].astype(BF16), x32, ln_g, ln_b, alpha)
    return x32, x16, vf


def _gdn_layer(x32, x16, p, batch, seq_len, alpha, ln_g, ln_b):
    m, d = x32.shape
    hv = p["a_log"].shape[0]
    value_dim = p["w_out"].shape[0]
    conv_ch = p["conv_w"].shape[1]
    key_dim = (conv_ch - value_dim) // 2
    w_in = p["w_in"]
    taps = p["conv_w"].shape[0]
    qkv = _matmul_conv(x16, w_in[:, :conv_ch].astype(BF16), _conv_table(p["conv_w"], None, conv_ch),
                       seq_len, taps, F32)
    zed = _matmul_stack(x16[None], w_in[None, :, conv_ch:conv_ch + value_dim].astype(BF16), F32)[0]
    gates, gates_t = _gdn_gates(x16, w_in[:, conv_ch + value_dim:].astype(BF16), p["a_log"], p["dt_bias"])
    rows = lambda t: t.reshape(hv, m // CHUNK, CHUNK).transpose(1, 0, 2).reshape(m // CHUNK, hv * CHUNK)
    b_rows = rows(gates_t[:hv])
    g_rows = rows(gates_t[hv:])
    norm_w4 = jnp.tile(_pad_rows(p["norm_w"].reshape(1, GDN_HEAD), V7X_SUBLANES), (1, 4))
    o = _gdn_chunk(qkv, zed, gates, g_rows, b_rows, norm_w4, batch, seq_len, key_dim, value_dim)
    return _matmul_res_ln(o, p["w_out"].astype(BF16), x32, ln_g, ln_b, alpha)


def _ffn_layer(x32, x16, p, seq_len, alpha, ln_g, ln_b):
    d_ff = p["w_down"].shape[0]
    ffp = _round_up(d_ff, 512 if d_ff >= 2048 else V7X_LANES)
    w_up, conv_w, conv_b = p["w_up"], p["conv_w"], p["conv_b"]
    wg = _pad_cols(w_up[:, :d_ff], ffp).astype(BF16)
    wu = _pad_cols(w_up[:, d_ff:], ffp).astype(BF16)
    tab_g = _conv_table(conv_w[:, :d_ff], conv_b[:d_ff], ffp)
    tab_u = _conv_table(conv_w[:, d_ff:], conv_b[d_ff:], ffp)
    act = _matmul_conv(x16, wg, tab_g, seq_len, conv_w.shape[0], BF16, w_up=wu, tab_up=tab_u)
    w_down = _pad_rows(p["w_down"], ffp).astype(BF16)
    return _matmul_res_ln(act, w_down, x32, ln_g, ln_b, alpha)


def kernel(x, rw_mu, rw_w_r, rw_w_k, rw_w_v, rw_w_o, rw_w0, rw_w1, rw_w2, rw_a0, rw_a1, rw_a2, rw_v0, rw_v1, rw_v2, rw_g1, rw_g2, rw_k_k, rw_k_a, rw_r_k, rw_lnx_g, rw_lnx_b, gdn_w_in, gdn_conv_w, gdn_a_log, gdn_dt_bias, gdn_norm_w, gdn_w_out, ffn_w_up, ffn_conv_w, ffn_conv_b, ffn_w_down, ln_mix_g, ln_mix_b, ln_ffn_g, ln_ffn_b):
    batch, seq_len, d = x.shape
    depth = ln_mix_g.shape[0]
    alpha = (2 * depth) ** 0.25
    x32 = x.reshape(batch * seq_len, d)
    x16 = None
    v_first = None
    for i in range(depth):
        j = i // 2
        if i % 2 == 0:
            p = dict(mu=rw_mu[j], w_r=rw_w_r[j], w_k=rw_w_k[j], w_v=rw_w_v[j], w_o=rw_w_o[j], w0=rw_w0[j],
                     w1=rw_w1[j], w2=rw_w2[j], a0=rw_a0[j], a1=rw_a1[j], a2=rw_a2[j], g1=rw_g1[j], g2=rw_g2[j],
                     k_k=rw_k_k[j], k_a=rw_k_a[j], r_k=rw_r_k[j], lnx_g=rw_lnx_g[j], lnx_b=rw_lnx_b[j])
            if j > 0:
                p.update(v0=rw_v0[j - 1], v1=rw_v1[j - 1], v2=rw_v2[j - 1])
            x32, x16, v_first = _rwkv_layer(x32, v_first, p, batch, seq_len, alpha, ln_mix_g[i], ln_mix_b[i])
        else:
            p = dict(w_in=gdn_w_in[j], conv_w=gdn_conv_w[j], a_log=gdn_a_log[j], dt_bias=gdn_dt_bias[j],
                     norm_w=gdn_norm_w[j], w_out=gdn_w_out[j])
            x32, x16 = _gdn_layer(x32, x16, p, batch, seq_len, alpha, ln_mix_g[i], ln_mix_b[i])
        p = dict(w_up=ffn_w_up[i], conv_w=ffn_conv_w[i], conv_b=ffn_conv_b[i], w_down=ffn_w_down[i])
        x32, x16 = _ffn_layer(x32, x16, p, seq_len, alpha, ln_ffn_g[i], ln_ffn_b[i])
    return x32.reshape(batch, seq_len, d)
```

```python
import functools
import math

import jax
import jax.numpy as jnp
from jax import lax
from jax.experimental import pallas as pl
from jax.experimental.pallas import tpu as pltpu

F32 = jnp.float32
BF16 = jnp.bfloat16

RW_HEAD = 64
GDN_HEAD = 128
LN_EPS = 1e-5
NORM_EPS = 1e-6
RW_LNX_EPS = 64e-5

V7X_LANES = 128
V7X_SUBLANES = 8
V7X_MXU_DIM = 256
V7X_VMEM_BYTES = 64 * 1024 * 1024
VMEM_LIMIT = V7X_VMEM_BYTES * 7 // 8

CHUNK = 64
GROUP = V7X_MXU_DIM
HALO = V7X_SUBLANES


def _params(*sem):
    return pltpu.CompilerParams(dimension_semantics=sem, vmem_limit_bytes=VMEM_LIMIT)


def _pick(n, prefs):
    for p in prefs:
        if p <= n and n % p == 0:
            return p
    return n


def _round_up(n, m):
    return (n + m - 1) // m * m


def _dot(a, b):
    return jnp.dot(a.astype(BF16), b.astype(BF16), preferred_element_type=F32)


def _dot_nt(a, b):
    return lax.dot_general(a.astype(BF16), b.astype(BF16), (((1,), (1,)), ((), ())),
                           preferred_element_type=F32)


def _dot_tn(a, b):
    return lax.dot_general(a.astype(BF16), b.astype(BF16), (((0,), (0,)), ((), ())),
                           preferred_element_type=F32)


def _split(x, terms):
    out = []
    rest = x
    for _ in range(terms):
        t = rest.astype(BF16)
        out.append(t)
        rest = rest - t.astype(F32)
    return out


def _dot_sel_left(sel, x, terms=3):
    acc = None
    for t in _split(x, terms):
        p = jnp.dot(sel, t, preferred_element_type=F32)
        acc = p if acc is None else acc + p
    return acc


def _dot_sel_right(x, sel, terms=3):
    acc = None
    for t in _split(x, terms):
        p = jnp.dot(t, sel, preferred_element_type=F32)
        acc = p if acc is None else acc + p
    return acc


def _iota(shape, dim):
    return lax.broadcasted_iota(jnp.int32, shape, dim)


def _seg(shape, dim, seg):
    return lax.shift_right_logical(_iota(shape, dim), int(math.log2(seg)))


def _pos(shape, dim, seg):
    return lax.bitwise_and(_iota(shape, dim), seg - 1)


def _block_ones(n, seg):
    return jnp.where(_seg((n, n), 0, seg) == _seg((n, n), 1, seg), 1.0, 0.0).astype(BF16)


def _block_tri(n, seg, upper):
    r, c = _iota((n, n), 0), _iota((n, n), 1)
    same = _seg((n, n), 0, seg) == _seg((n, n), 1, seg)
    tri = (r <= c) if upper else (r >= c)
    return jnp.where(jnp.logical_and(same, tri), 1.0, 0.0).astype(BF16)


def _segsum(x, ones_blk):
    return _dot_sel_right(x, ones_blk, terms=1)


def _bd_mask(rows, lanes, row_seg, lane_seg):
    return _seg((rows, lanes), 0, row_seg) == _seg((rows, lanes), 1, lane_seg)


def _block_diag(x, mask):
    x16 = x.astype(BF16)
    tall = jnp.concatenate([x16] * (mask.shape[0] // x.shape[0]), axis=0)
    return jnp.where(mask, tall, jnp.zeros_like(tall))


def _sigmoid(x):
    return 0.5 + 0.5 * jnp.tanh(0.5 * x)


def _layer_norm(y, g, b):
    mu = jnp.mean(y, axis=-1, keepdims=True)
    yc = y - mu
    var = jnp.mean(yc * yc, axis=-1, keepdims=True)
    return yc * lax.rsqrt(var + LN_EPS) * g + b


def _mm_kernel(x_ref, w_ref, o_ref):
    o_ref[...] = jnp.dot(x_ref[...], w_ref[...], preferred_element_type=F32).astype(o_ref.dtype)


def _matmul_stack(xs, ws, out_dtype):
    p, m, k = xs.shape
    n = ws.shape[2]
    bm = _pick(m, (1024, 512, 256, 128))
    bn = _pick(n, (1024, 512, 256, 128))
    return pl.pallas_call(
        _mm_kernel,
        grid=(p, n // bn, m // bm),
        in_specs=[pl.BlockSpec((None, bm, k), lambda q, j, i: (q, i, 0)),
                  pl.BlockSpec((None, k, bn), lambda q, j, i: (q, 0, j))],
        out_specs=pl.BlockSpec((None, bm, bn), lambda q, j, i: (q, i, j)),
        out_shape=jax.ShapeDtypeStruct((p, m, n), out_dtype),
        compiler_params=_params("parallel", "parallel", "parallel"),
        name="matmul_stack",
    )(xs, ws)


def _mm_res_ln_kernel(a_ref, w_ref, res_ref, gb_ref, o32_ref, o16_ref, *, alpha, nk, sub):
    kk = pl.program_id(1)
    blocks = [slice(s * sub, (s + 1) * sub) for s in range(a_ref.shape[0] // sub)]
    part = lambda rows: jnp.dot(a_ref[rows, :], w_ref[...], preferred_element_type=F32)

    @pl.when(kk == 0)
    def _():
        for rows in blocks:
            o32_ref[rows, :] = part(rows)

    @pl.when(kk > 0)
    def _():
        for rows in blocks:
            o32_ref[rows, :] += part(rows)

    @pl.when(kk == nk - 1)
    def _():
        for rows in blocks:
            y = alpha * res_ref[rows, :] + o32_ref[rows, :]
            out = _layer_norm(y, gb_ref[0:1, :], gb_ref[1:2, :])
            o32_ref[rows, :] = out
            o16_ref[rows, :] = out.astype(BF16)


def _matmul_res_ln(a, w, resid, gamma, beta, alpha):
    m, k = a.shape
    d = w.shape[1]
    bm = _pick(m, (1024, 512, 256, 128))
    bk = _pick(k, (512, 256, 128))
    nk = k // bk
    gb = jnp.concatenate([gamma.reshape(1, d), beta.reshape(1, d),
                          jnp.zeros((V7X_SUBLANES - 2, d), F32)], axis=0)
    return pl.pallas_call(
        functools.partial(_mm_res_ln_kernel, alpha=alpha, nk=nk, sub=_pick(bm, (512, 256, 128))),
        grid=(m // bm, nk),
        in_specs=[pl.BlockSpec((bm, bk), lambda i, kk: (i, kk)),
                  pl.BlockSpec((bk, d), lambda i, kk: (kk, 0)),
                  pl.BlockSpec((bm, d), lambda i, kk: (i, 0)),
                  pl.BlockSpec((V7X_SUBLANES, d), lambda i, kk: (0, 0))],
        out_specs=[pl.BlockSpec((bm, d), lambda i, kk: (i, 0)),
                   pl.BlockSpec((bm, d), lambda i, kk: (i, 0))],
        out_shape=[jax.ShapeDtypeStruct((m, d), F32), jax.ShapeDtypeStruct((m, d), BF16)],
        compiler_params=_params("parallel", "arbitrary"),
        name="matmul_res_ln",
    )(a, w, resid, gb)


def _shift_rows(h, prev, shift):
    rolled = pltpu.roll(h, shift=shift, axis=0)
    head = jnp.where(_iota(prev.shape, 0) < shift, pltpu.roll(prev, shift=shift, axis=0), rolled[0:HALO])
    return jnp.concatenate([head, rolled[HALO:]], axis=0)


def _mm_conv_kernel(x_ref, xh_ref, *refs, taps, glu, seq_len, bm):
    if glu:
        wg_ref, wu_ref, cg_ref, cu_ref, o_ref = refs
    else:
        wg_ref, cg_ref, o_ref = refs
    i = pl.program_id(1)
    valid = jnp.where((i * bm) % seq_len == 0, 0.0, 1.0).astype(F32)
    x = x_ref[...]
    xh = xh_ref[...]

    def branch(w_ref, c_ref):
        h = jnp.dot(x, w_ref[...], preferred_element_type=F32)
        hh = jnp.dot(xh, w_ref[...], preferred_element_type=F32) * valid
        c = c_ref[...]
        out = c[taps:taps + 1, :] + c[taps - 1:taps, :] * h
        for k in range(taps - 1):
            out = out + c[k:k + 1, :] * _shift_rows(h, hh, taps - 1 - k)
        return out

    half = 0.5 * branch(wg_ref, cg_ref)
    act = half + half * jnp.tanh(half)
    if glu:
        act = act * branch(wu_ref, cu_ref)
    o_ref[...] = act.astype(o_ref.dtype)


def _conv_table(conv_w, bias, n_pad):
    taps, n = conv_w.shape
    rows = [conv_w, (jnp.zeros((1, n), F32) if bias is None else bias.reshape(1, n)),
            jnp.zeros((V7X_SUBLANES - taps - 1, n), F32)]
    tab = jnp.concatenate(rows, axis=0)
    return jnp.pad(tab, ((0, 0), (0, n_pad - n)))


def _matmul_conv(x16, w, tab, seq_len, taps, out_dtype, w_up=None, tab_up=None):
    m, k = x16.shape
    n = w.shape[1]
    glu = w_up is not None
    bm = _pick(seq_len, ((1024,) if glu else (2048, 1024)) + (512, 256, 128, 64, 32, 16, 8))
    bn = _pick(n, (512, 256, 128))
    hb = bm // HALO
    x_spec = pl.BlockSpec((bm, k), lambda j, i: (i, 0))
    xh_spec = pl.BlockSpec((HALO, k), lambda j, i: (jnp.maximum(i * hb - 1, 0), 0))
    w_spec = pl.BlockSpec((k, bn), lambda j, i: (0, j))
    c_spec = pl.BlockSpec((V7X_SUBLANES, bn), lambda j, i: (0, j))
    if glu:
        in_specs = [x_spec, xh_spec, w_spec, w_spec, c_spec, c_spec]
        args = (x16, x16, w, w_up, tab, tab_up)
    else:
        in_specs = [x_spec, xh_spec, w_spec, c_spec]
        args = (x16, x16, w, tab)
    return pl.pallas_call(
        functools.partial(_mm_conv_kernel, taps=taps, glu=glu, seq_len=seq_len, bm=bm),
        grid=(n // bn, m // bm),
        in_specs=in_specs,
        out_specs=pl.BlockSpec((bm, bn), lambda j, i: (i, j)),
        out_shape=jax.ShapeDtypeStruct((m, n), out_dtype),
        compiler_params=_params("parallel", "parallel"),
        name="matmul_conv_glu" if glu else "matmul_conv",
    )(*args)


def _rwkv_in_kernel(x_ref, xh_ref, mu_ref, wd_ref, xm_ref, lh_ref, *, seq_len, bm, has_vres):
    i = pl.program_id(0)
    valid = jnp.where((i * bm) % seq_len == 0, 0.0, 1.0).astype(F32)
    x = x_ref[...]
    prev_row = xh_ref[HALO - 1:HALO, :] * valid
    rolled = pltpu.roll(x, shift=1, axis=0)
    x_prev = jnp.where(_iota(x.shape, 0) == 0, prev_row, rolled)
    xx = x_prev - x
    mu = mu_ref[...]
    mix = lambda r: (x + xx * mu[r:r + 1, :]).astype(BF16)
    xm_ref[0] = mix(0)
    xm_ref[1] = mix(2)
    xv = mix(3)
    xm_ref[2] = xv
    lp = V7X_LANES
    down = lambda xin, lo, hi: jnp.dot(xin, wd_ref[:, lo:hi], preferred_element_type=F32)
    lh_ref[:, 0:lp] = jnp.tanh(down(mix(1), 0, lp)).astype(BF16)
    lh_ref[:, lp:2 * lp] = down(mix(4), lp, 2 * lp).astype(BF16)
    if has_vres:
        lh_ref[:, 2 * lp:3 * lp] = down(xv, 2 * lp, 3 * lp).astype(BF16)
    else:
        lh_ref[:, 2 * lp:3 * lp] = jnp.zeros((bm, lp), BF16)
    lh_ref[:, 3 * lp:] = jax.nn.sigmoid(down(mix(5), 3 * lp, wd_ref.shape[1])).astype(BF16)


def _rwkv_in(x, mu8, wd, seq_len, has_vres):
    m, d = x.shape
    lh = wd.shape[1]
    bm = _pick(seq_len, (512, 256, 128, 64, 32, 16, 8))
    hb = bm // HALO
    return pl.pallas_call(
        functools.partial(_rwkv_in_kernel, seq_len=seq_len, bm=bm, has_vres=has_vres),
        grid=(m // bm,),
        in_specs=[pl.BlockSpec((bm, d), lambda i: (i, 0)),
                  pl.BlockSpec((HALO, d), lambda i: (jnp.maximum(i * hb - 1, 0), 0)),
                  pl.BlockSpec((V7X_SUBLANES, d), lambda i: (0, 0)),
                  pl.BlockSpec((d, lh), lambda i: (0, 0))],
        out_specs=[pl.BlockSpec((3, bm, d), lambda i: (0, i, 0)),
                   pl.BlockSpec((bm, lh), lambda i: (i, 0))],
        out_shape=[jax.ShapeDtypeStruct((3, m, d), BF16), jax.ShapeDtypeStruct((m, lh), BF16)],
        compiler_params=_params("parallel"),
        name="rwkv_in",
    )(x, x, mu8, wd)


def _lockstep(chains):
    chains = list(chains)
    while chains:
        alive = []
        for ch in chains:
            try:
                next(ch)
                alive.append(ch)
            except StopIteration:
                pass
        chains = alive


def _neumann_chain(nmat, eye_l, bd, out):
    c = nmat.shape[0]
    p = eye_l + nmat
    nk = _dot(nmat, bd(nmat))
    yield
    n = 2
    while 2 * n < c:
        both = _dot(jnp.concatenate([p, nk], axis=0), bd(nk))
        yield
        p = p + both[:c]
        nk = both[c:]
        n *= 2
    out["tm"] = p + _dot(p, bd(nk))
    yield


def _rwkv_chunk_kernel(r_ref, k_ref, v_ref, vf_ref, lh_ref, wup_ref, g2_ref, vec_ref, z_ref,
                       ht_ref, *, has_vres, n_chunks, n_groups):
    c = CHUNK
    hd = RW_HEAD
    lp = V7X_LANES

    @pl.when(pl.program_id(2) == 0)
    def _():
        ht_ref[...] = jnp.zeros(ht_ref.shape, F32)

    ones_blk = _block_ones(GROUP, hd)
    segsum = lambda t: _segsum(t, ones_blk)
    row = _iota((c, GROUP), 0)
    col = _pos((c, GROUP), 1, c)
    strict = row > col
    incl = row >= col
    eye_l = jnp.where(row == col, 1.0, 0.0).astype(F32)
    mask = _bd_mask(GROUP, GROUP, c, hd)
    bd = lambda t: _block_diag(t, mask)
    tri = _block_tri(n_chunks * c, c, upper=False)

    grp = []
    for g in range(n_groups):
        ln = slice(g * GROUP, (g + 1) * GROUP)
        vec = vec_ref[:, ln]
        w0, a0, v0, k_k, k_a, r_k, lnx_g, lnx_b = [vec[i:i + 1, :] for i in range(8)]
        r = r_ref[:, ln]
        k = k_ref[:, ln]
        v = v_ref[:, ln]
        up = lambda lo, hi, w: jnp.dot(lh_ref[:, lo:hi], w, preferred_element_type=F32)
        lw = -math.exp(-0.5) * _sigmoid(w0 + up(0, lp, wup_ref[0, :, ln]))
        a = _sigmoid(a0 + up(lp, 2 * lp, wup_ref[1, :, ln]))
        if has_vres:
            v = v + (vf_ref[:, ln] - v) * _sigmoid(v0 + up(2 * lp, 3 * lp, wup_ref[2, :, ln]))
        gate = up(3 * lp, lh_ref.shape[1], g2_ref[:, ln])
        kk = k * k_k
        kk = kk * lax.rsqrt(segsum(kk * kk) + NORM_EPS)
        k2 = k * (1.0 + (a - 1.0) * k_a)
        bv = kk * a
        cum = _dot_sel_left(tri, lw, terms=2)
        e_inv = jnp.exp(-cum)
        grp.append(dict(r=r, k2=k2, v=v, bv=bv, cum=cum, gate=gate, r_k=r_k, lnx_g=lnx_g, lnx_b=lnx_b,
                        e_inv=e_inv, at=-kk * jnp.exp(cum - lw), rt=r * jnp.exp(cum), bt=bv * e_inv,
                        kt=k2 * e_inv))

    res = {}

    def prepare(g, ci):
        d = grp[g]
        sl = slice(ci * c, (ci + 1) * c)
        vc, at, rt = d["v"][sl], d["at"][sl], d["rt"][sl]
        decay = jnp.exp(d["cum"][(ci + 1) * c - 1:(ci + 1) * c, :])
        e_end = decay * d["e_inv"][sl]
        bk_end = jnp.concatenate([d["bv"][sl] * e_end, d["k2"][sl] * e_end], axis=0)
        lhs = jnp.concatenate([at, rt], axis=0)
        xb = _dot_nt(lhs, bd(d["bt"][sl]))
        xk = _dot_nt(lhs, bd(d["kt"][sl]))
        yield
        aab = jnp.where(strict, xb[:c], 0.0)
        prb = jnp.where(incl, xb[c:], 0.0)
        aak = jnp.where(strict, xk[:c], 0.0)
        prk = jnp.where(incl, xk[c:], 0.0)
        xv = _dot(jnp.concatenate([aak, prk], axis=0), bd(vc))
        inv = {}
        yield from _neumann_chain(aab, eye_l, bd, inv)
        tm = inv["tm"]
        wmat = _dot(tm, bd(at))
        u0 = _dot(tm, bd(xv[:c]))
        yield
        res[g, ci] = dict(lhs=jnp.concatenate([wmat, rt], axis=0), u0=u0, prb=prb, y0=xv[c:], vc=vc,
                          bk_end=bk_end, decay=decay)

    ys = [[] for _ in range(n_groups)]

    def advance(g):
        ht = ht_ref[g]
        for ci in range(n_chunks):
            d = res[g, ci]
            xh = _dot_nt(d["lhs"], ht)
            yield
            u = xh[:c] + d["u0"]
            upd = _dot_tn(jnp.concatenate([u, d["vc"]], axis=0), d["bk_end"])
            ys[g].append(xh[c:] + _dot(d["prb"], bd(u)) + d["y0"])
            yield
            ht = d["decay"] * ht + jnp.where(mask, upd, 0.0)
        ht_ref[g] = ht

    _lockstep(prepare(g, ci) for ci in range(n_chunks) for g in range(n_groups))
    _lockstep(advance(g) for g in range(n_groups))

    inv_n = 1.0 / hd
    for g in range(n_groups):
        d = grp[g]
        y = jnp.concatenate(ys[g], axis=0)
        mean = segsum(y) * inv_n
        yc = y - mean
        var = segsum(yc * yc) * inv_n
        yn = yc * lax.rsqrt(var + RW_LNX_EPS) * d["lnx_g"] + d["lnx_b"]
        bonus = segsum(d["r"] * d["k2"] * d["r_k"]) * d["v"]
        z_ref[:, g * GROUP:(g + 1) * GROUP] = ((yn + bonus) * d["gate"]).astype(z_ref.dtype)


def _rwkv_chunk(rkv, v_first, lh, wup, g2, vecs, batch, seq_len, has_vres):
    _, m, d = rkv.shape
    lhw = lh.shape[1]
    tb = _pick(seq_len, (128, 64))
    nt = seq_len // tb
    n_groups = _pick(d // GROUP, (8, 4, 2, 1))
    lanes = n_groups * GROUP
    row = lambda b, j, t: b * nt + t
    plane = lambda p: pl.BlockSpec((None, tb, lanes), lambda b, j, t: (p, row(b, j, t), j))
    return pl.pallas_call(
        functools.partial(_rwkv_chunk_kernel, has_vres=has_vres, n_chunks=tb // CHUNK, n_groups=n_groups),
        grid=(batch, d // lanes, nt),
        in_specs=[plane(0), plane(1), plane(2), plane(2),
                  pl.BlockSpec((tb, lhw), lambda b, j, t: (row(b, j, t), 0)),
                  pl.BlockSpec((3, V7X_LANES, lanes), lambda b, j, t: (0, 0, j)),
                  pl.BlockSpec((g2.shape[0], lanes), lambda b, j, t: (0, j)),
                  pl.BlockSpec((V7X_SUBLANES, lanes), lambda b, j, t: (0, j))],
        out_specs=pl.BlockSpec((tb, lanes), lambda b, j, t: (row(b, j, t), j)),
        out_shape=jax.ShapeDtypeStruct((m, d), BF16),
        scratch_shapes=[pltpu.VMEM((n_groups, GROUP, GROUP), F32)],
        compiler_params=_params("parallel", "parallel", "arbitrary"),
        name="rwkv_chunk",
    )(rkv, rkv, rkv, v_first, lh, wup, g2, vecs)


def _gdn_gates_kernel(x_ref, w_ref, wt_ref, ad_ref, adt_ref, gb_ref, gt_ref, *, hv, bm):
    c = CHUNK
    x = x_ref[...]
    ba = jnp.dot(x, w_ref[...], preferred_element_type=F32)
    bat = lax.dot_general(wt_ref[...], x, (((1,), (1,)), ((), ())),
                          preferred_element_type=F32)
    ad = ad_ref[...]
    adt = adt_ref[...]
    g = -jnp.exp(ad[0:1, :]) * jax.nn.softplus(ba[:, hv:] + ad[1:2, :])
    gt = -jnp.exp(adt[:, 0:1]) * jax.nn.softplus(bat[hv:, :] + adt[:, 1:2])
    gb_ref[:, 0:hv] = jax.nn.sigmoid(ba[:, :hv])
    gt_ref[0:hv, :] = jax.nn.sigmoid(bat[:hv, :])
    gb_ref[:, hv:] = _dot_sel_left(_block_tri(bm, c, upper=False), g)
    gt_ref[hv:, :] = _dot_sel_right(gt, _block_tri(bm, c, upper=True))


def _gdn_gates(x16, w_ba, a_log, dt_bias):
    m, d = x16.shape
    hv = a_log.shape[0]
    bm = _pick(m, (512, 256, 128))
    ad = jnp.concatenate([a_log.reshape(1, hv), dt_bias.reshape(1, hv),
                          jnp.zeros((V7X_SUBLANES - 2, hv), F32)], axis=0)
    adt = jnp.transpose(ad)
    return pl.pallas_call(
        functools.partial(_gdn_gates_kernel, hv=hv, bm=bm),
        grid=(m // bm,),
        in_specs=[pl.BlockSpec((bm, d), lambda i: (i, 0)),
                  pl.BlockSpec((d, 2 * hv), lambda i: (0, 0)),
                  pl.BlockSpec((2 * hv, d), lambda i: (0, 0)),
                  pl.BlockSpec((V7X_SUBLANES, hv), lambda i: (0, 0)),
                  pl.BlockSpec((hv, V7X_SUBLANES), lambda i: (0, 0))],
        out_specs=[pl.BlockSpec((bm, 2 * hv), lambda i: (i, 0)),
                   pl.BlockSpec((2 * hv, bm), lambda i: (0, i))],
        out_shape=[jax.ShapeDtypeStruct((m, 2 * hv), F32), jax.ShapeDtypeStruct((2 * hv, m), F32)],
        compiler_params=_params("parallel"),
        name="gdn_gates",
    )(x16, w_ba, jnp.transpose(w_ba), ad, adt)


def _gdn_chunk_kernel(q_ref, k_ref, v_ref, z_ref, gb_ref, grow_ref, brow_ref, nw_ref, o_ref,
                      s_ref, *, hv, n_chunks, n_groups):
    c = CHUNK
    hd = GDN_HEAD
    vh = GROUP // CHUNK
    j = pl.program_id(1)

    @pl.when(pl.program_id(2) == 0)
    def _():
        s_ref[...] = jnp.zeros(s_ref.shape, F32)

    ones_blk = _block_ones(2 * hd, hd)
    segsum = lambda t: _segsum(t, ones_blk)
    l2 = lambda t: t * lax.rsqrt(segsum(t * t) + NORM_EPS)
    row = _iota((c, vh * c), 0)
    col = _pos((c, vh * c), 1, c)
    strict = row > col
    incl = row >= col
    eye_l = jnp.where(row == col, 1.0, 0.0).astype(F32)
    kmask = _bd_mask(vh * c, 2 * hd, 2 * c, hd)
    pair_mask = _bd_mask(2 * hd, 2 * hd, hd, hd)
    cc_mask = _bd_mask(vh * c, vh * c, c, c)
    wide_mask = _bd_mask(vh * c, vh * hd, c, hd)
    rep = lambda t: jnp.concatenate([t[:, :hd], t[:, :hd], t[:, hd:], t[:, hd:]], axis=1)
    bd = lambda t: _block_diag(t, wide_mask)
    bd_cc = lambda t: _block_diag(t, cc_mask)
    pairs = [slice(p * 2 * hd, (p + 1) * 2 * hd) for p in range(2)]

    gb = gb_ref[...]
    src = _iota((2 * hv, vh * c), 0)
    src2 = _iota((2 * hv, vh * hd), 0)
    grp = []
    for g in range(n_groups):
        first = vh * (n_groups * j + g)
        sel64 = lambda base: jnp.where(src == base + first + _seg((2 * hv, vh * c), 1, c),
                                       1.0, 0.0).astype(BF16)
        sel128 = jnp.where(src2 == hv + first + _seg((2 * hv, vh * hd), 1, hd), 1.0, 0.0).astype(BF16)
        gcol128 = _dot_sel_right(gb, sel128, terms=2)
        grp.append(dict(q=l2(q_ref[:, g * 2 * hd:(g + 1) * 2 * hd]) * (hd ** -0.5),
                        k=l2(k_ref[:, g * 2 * hd:(g + 1) * 2 * hd]),
                        v=v_ref[:, g * vh * hd:(g + 1) * vh * hd],
                        bcol=_dot_sel_right(gb, sel64(0), terms=1),
                        gcol=_dot_sel_right(gb, sel64(hv), terms=2),
                        gcol128=gcol128, e_g=jnp.exp(gcol128)))

    res = {}

    def prepare(g, ci):
        d = grp[g]
        sl = slice(ci * c, (ci + 1) * c)
        qc, kc, vc = d["q"][sl], d["k"][sl], d["v"][sl]
        gcol128 = d["gcol128"][sl]
        grow = grow_ref[ci:ci + 1, g * vh * c:(g + 1) * vh * c]
        brow = brow_ref[ci:ci + 1, g * vh * c:(g + 1) * vh * c]
        decay = jnp.exp(jnp.where(incl, d["gcol"][sl] - grow, 0.0))
        both = _dot_nt(jnp.concatenate([kc, qc], axis=0), _block_diag(kc, kmask))
        yield
        amat = jnp.where(strict, both[:c] * decay * d["bcol"][sl], 0.0)
        attn = jnp.where(incl, both[c:] * decay, 0.0)
        inv = {}
        yield from _neumann_chain(-amat, eye_l, bd_cc, inv)
        tm = inv["tm"]
        k4 = rep(kc)
        u = _dot(tm * brow, bd(vc))
        wk = _dot(tm * (brow * jnp.exp(grow)), bd(k4))
        yield
        glast = gcol128[c - 1:c, :]
        q4 = rep(qc)
        res[g, ci] = dict(lhs=[jnp.concatenate([wk[:, ln], q4[:, ln]], axis=0) for ln in pairs],
                          u=u, attn=attn, k4=k4, tail=jnp.exp(glast - gcol128), e_last=jnp.exp(glast),
                          e_g=d["e_g"][sl])

    os_ = [[] for _ in range(n_groups)]

    def advance(g):
        s_pair = [s_ref[g, 0], s_ref[g, 1]]
        for ci in range(n_chunks):
            d = res[g, ci]
            ws = [_dot(d["lhs"][p], s_pair[p]) for p in range(2)]
            yield
            v_new = d["u"] - jnp.concatenate([w[:c] for w in ws], axis=1)
            vs = v_new * d["tail"]
            upd = [_dot_tn(d["k4"][:, ln], vs[:, ln]) for ln in pairs]
            os_[g].append(jnp.concatenate([w[c:] for w in ws], axis=1) * d["e_g"] + _dot(d["attn"], bd(v_new)))
            yield
            s_pair = [s_pair[p] * d["e_last"][:, pairs[p]] + jnp.where(pair_mask, upd[p], 0.0)
                      for p in range(2)]
        s_ref[g, 0] = s_pair[0]
        s_ref[g, 1] = s_pair[1]

    _lockstep(prepare(g, ci) for ci in range(n_chunks) for g in range(n_groups))
    _lockstep(advance(g) for g in range(n_groups))

    for g in range(n_groups):
        ln = slice(g * vh * hd, (g + 1) * vh * hd)
        o = jnp.concatenate(os_[g], axis=0)
        oo = o * o
        ms = jnp.concatenate([segsum(oo[:, pr]) for pr in pairs], axis=1) * (1.0 / hd)
        z = z_ref[:, ln]
        o_ref[:, ln] = (o * lax.rsqrt(ms + NORM_EPS) * nw_ref[0:1, :] * (z * _sigmoid(z))).astype(o_ref.dtype)


def _gdn_chunk(qkv, zed, gates, g_rows, b_rows, norm_w4, batch, seq_len, key_dim, value_dim):
    m = qkv.shape[0]
    hv = gates.shape[1] // 2
    hd = GDN_HEAD
    tb = _pick(seq_len, (128, 64))
    nt = seq_len // tb
    n_groups = _pick(value_dim // (4 * hd), (8, 4, 2, 1))
    kw = n_groups * 2 * hd
    vw = n_groups * 4 * hd
    kb = key_dim // kw
    vb = 2 * key_dim // vw
    row = lambda b, j, t: b * nt + t
    rows_per_tb = tb // CHUNK
    g_rows = g_rows.reshape(m // tb, rows_per_tb, hv * CHUNK)
    b_rows = b_rows.reshape(m // tb, rows_per_tb, hv * CHUNK)
    rows_spec = pl.BlockSpec((None, rows_per_tb, n_groups * 4 * CHUNK), lambda b, j, t: (row(b, j, t), 0, j))
    return pl.pallas_call(
        functools.partial(_gdn_chunk_kernel, hv=hv, n_chunks=tb // CHUNK, n_groups=n_groups),
        grid=(batch, value_dim // vw, nt),
        in_specs=[pl.BlockSpec((tb, kw), lambda b, j, t: (row(b, j, t), j)),
                  pl.BlockSpec((tb, kw), lambda b, j, t: (row(b, j, t), kb + j)),
                  pl.BlockSpec((tb, vw), lambda b, j, t: (row(b, j, t), vb + j)),
                  pl.BlockSpec((tb, vw), lambda b, j, t: (row(b, j, t), j)),
                  pl.BlockSpec((tb, 2 * hv), lambda b, j, t: (row(b, j, t), 0)),
                  rows_spec, rows_spec,
                  pl.BlockSpec((V7X_SUBLANES, 4 * hd), lambda b, j, t: (0, 0))],
        out_specs=pl.BlockSpec((tb, vw), lambda b, j, t: (row(b, j, t), j)),
        out_shape=jax.ShapeDtypeStruct((m, value_dim), BF16),
        scratch_shapes=[pltpu.VMEM((n_groups, 2, 2 * hd, 2 * hd), F32)],
        compiler_params=_params("parallel", "parallel", "arbitrary"),
        name="gdn_chunk",
    )(qkv, qkv, qkv, zed, gates, g_rows, b_rows, norm_w4)


def _pad_rows(w, rows):
    return jnp.pad(w, ((0, rows - w.shape[0]), (0, 0)))


def _pad_cols(w, cols):
    return jnp.pad(w, ((0, 0), (0, cols - w.shape[1])))


def _rwkv_layer(x32, v_first, p, batch, seq_len, alpha, ln_g, ln_b):
    m, d = x32.shape
    lp = V7X_LANES
    has_vres = v_first is not None
    mu8 = _pad_rows(p["mu"], V7X_SUBLANES)
    v1 = p["v1"] if has_vres else jnp.zeros((d, lp), F32)
    wd = jnp.concatenate([_pad_cols(p["w1"], lp), _pad_cols(p["a1"], lp), _pad_cols(v1, lp), p["g1"]],
                         axis=1).astype(BF16)
    xm, lh = _rwkv_in(x32, mu8, wd, seq_len, has_vres)
    ws = jnp.stack([p["w_r"], p["w_k"], p["w_v"]]).astype(BF16)
    rkv = _matmul_stack(xm, ws, F32)
    v2 = p["v2"] if has_vres else jnp.zeros((lp, d), F32)
    wup = jnp.stack([_pad_rows(p["w2"], lp), _pad_rows(p["a2"], lp), _pad_rows(v2, lp)]).astype(BF16)
    v0 = p["v0"] if has_vres else jnp.zeros((d,), F32)
    vecs = jnp.stack([p["w0"], p["a0"], v0, p["k_k"], p["k_a"], p["r_k"].reshape(d), p["lnx_g"], p["lnx_b"]])
    vf = v_first if has_vres else rkv
    z = _rwkv_chunk(rkv, vf, lh, wup, p["g2"].astype(BF16), vecs, batch, seq_len, has_vres)
    x32, x16 = _matmul_res_ln(z, p["w_o"].astype(BF16), x32, ln_g, ln_b, alpha)
    return x32, x16, vf


def _gdn_layer(x32, x16, p, batch, seq_len, alpha, ln_g, ln_b):
    m, d = x32.shape
    hv = p["a_log"].shape[0]
    value_dim = p["w_out"].shape[0]
    conv_ch = p["conv_w"].shape[1]
    key_dim = (conv_ch - value_dim) // 2
    w_in = p["w_in"]
    taps = p["conv_w"].shape[0]
    qkv = _matmul_conv(x16, w_in[:, :conv_ch].astype(BF16), _conv_table(p["conv_w"], None, conv_ch),
                       seq_len, taps, F32)
    zed = _matmul_stack(x16[None], w_in[None, :, conv_ch:conv_ch + value_dim].astype(BF16), F32)[0]
    gates, gates_t = _gdn_gates(x16, w_in[:, conv_ch + value_dim:].astype(BF16), p["a_log"], p["dt_bias"])
    rows = lambda t: t.reshape(hv, m // CHUNK, CHUNK).transpose(1, 0, 2).reshape(m // CHUNK, hv * CHUNK)
    b_rows = rows(gates_t[:hv])
    g_rows = rows(gates_t[hv:])
    norm_w4 = jnp.tile(_pad_rows(p["norm_w"].reshape(1, GDN_HEAD), V7X_SUBLANES), (1, 4))
    o = _gdn_chunk(qkv, zed, gates, g_rows, b_rows, norm_w4, batch, seq_len, key_dim, value_dim)
    return _matmul_res_ln(o, p["w_out"].astype(BF16), x32, ln_g, ln_b, alpha)


def _ffn_layer(x32, x16, p, seq_len, alpha, ln_g, ln_b):
    d_ff = p["w_down"].shape[0]
    ffp = _round_up(d_ff, 512 if d_ff >= 2048 else V7X_LANES)
    w_up, conv_w, conv_b = p["w_up"], p["conv_w"], p["conv_b"]
    wg = _pad_cols(w_up[:, :d_ff], ffp).astype(BF16)
    wu = _pad_cols(w_up[:, d_ff:], ffp).astype(BF16)
    tab_g = _conv_table(conv_w[:, :d_ff], conv_b[:d_ff], ffp)
    tab_u = _conv_table(conv_w[:, d_ff:], conv_b[d_ff:], ffp)
    act = _matmul_conv(x16, wg, tab_g, seq_len, conv_w.shape[0], BF16, w_up=wu, tab_up=tab_u)
    w_down = _pad_rows(p["w_down"], ffp).astype(BF16)
    return _matmul_res_ln(act, w_down, x32, ln_g, ln_b, alpha)


def kernel(x, rw_mu, rw_w_r, rw_w_k, rw_w_v, rw_w_o, rw_w0, rw_w1, rw_w2, rw_a0, rw_a1, rw_a2, rw_v0, rw_v1, rw_v2, rw_g1, rw_g2, rw_k_k, rw_k_a, rw_r_k, rw_lnx_g, rw_lnx_b, gdn_w_in, gdn_conv_w, gdn_a_log, gdn_dt_bias, gdn_norm_w, gdn_w_out, ffn_w_up, ffn_conv_w, ffn_conv_b, ffn_w_down, ln_mix_g, ln_mix_b, ln_ffn_g, ln_ffn_b):
    batch, seq_len, d = x.shape
    depth = ln_mix_g.shape[0]
    alpha = (2 * depth) ** 0.25
    x32 = x.reshape(batch * seq_len, d)
    x16 = None
    v_first = None
    for i in range(depth):
        j = i // 2
        if i % 2 == 0:
            p = dict(mu=rw_mu[j], w_r=rw_w_r[j], w_k=rw_w_k[j], w_v=rw_w_v[j], w_o=rw_w_o[j], w0=rw_w0[j],
                     w1=rw_w1[j], w2=rw_w2[j], a0=rw_a0[j], a1=rw_a1[j], a2=rw_a2[j], g1=rw_g1[j], g2=rw_g2[j],
                     k_k=rw_k_k[j], k_a=rw_k_a[j], r_k=rw_r_k[j], lnx_g=rw_lnx_g[j], lnx_b=rw_lnx_b[j])
            if j > 0:
                p.update(v0=rw_v0[j - 1], v1=rw_v1[j - 1], v2=rw_v2[j - 1])
            x32, x16, v_first = _rwkv_layer(x32, v_first, p, batch, seq_len, alpha, ln_mix_g[i], ln_mix_b[i])
        else:
            p = dict(w_in=gdn_w_in[j], conv_w=gdn_conv_w[j], a_log=gdn_a_log[j], dt_bias=gdn_dt_bias[j],
                     norm_w=gdn_norm_w[j], w_out=gdn_w_out[j])
            x32, x16 = _gdn_layer(x32, x16, p, batch, seq_len, alpha, ln_mix_g[i], ln_mix_b[i])
        p = dict(w_up=ffn_w_up[i], conv_w=ffn_conv_w[i], conv_b=ffn_conv_b[i], w_down=ffn_w_down[i])
        x32, x16 = _ffn_layer(x32, x16, p, seq_len, alpha, ln_ffn_g[i], ln_ffn_b[i])
    return x32.reshape(batch, seq_len, d)
```

```python
import functools
import math

import jax
import jax.numpy as jnp
from jax import lax
from jax.experimental import pallas as pl
from jax.experimental.pallas import tpu as pltpu

F32 = jnp.float32
BF16 = jnp.bfloat16

RW_HEAD = 64
GDN_HEAD = 128
LN_EPS = 1e-5
NORM_EPS = 1e-6
RW_LNX_EPS = 64e-5

V7X_LANES = 128
V7X_SUBLANES = 8
V7X_MXU_DIM = 256
V7X_VMEM_BYTES = 64 * 1024 * 1024
VMEM_LIMIT = V7X_VMEM_BYTES * 7 // 8

CHUNK = 64
GROUP = V7X_MXU_DIM
HALO = V7X_SUBLANES


def _params(*sem):
    return pltpu.CompilerParams(dimension_semantics=sem, vmem_limit_bytes=VMEM_LIMIT)


def _pick(n, prefs):
    for p in prefs:
        if p <= n and n % p == 0:
            return p
    return n


def _round_up(n, m):
    return (n + m - 1) // m * m


def _dot(a, b):
    return jnp.dot(a.astype(BF16), b.astype(BF16), preferred_element_type=F32)


def _dot_nt(a, b):
    return lax.dot_general(a.astype(BF16), b.astype(BF16), (((1,), (1,)), ((), ())),
                           preferred_element_type=F32)


def _dot_tn(a, b):
    return lax.dot_general(a.astype(BF16), b.astype(BF16), (((0,), (0,)), ((), ())),
                           preferred_element_type=F32)


def _split(x, terms):
    out = []
    rest = x
    for _ in range(terms):
        t = rest.astype(BF16)
        out.append(t)
        rest = rest - t.astype(F32)
    return out


def _dot_sel_left(sel, x, terms=3):
    acc = None
    for t in _split(x, terms):
        p = jnp.dot(sel, t, preferred_element_type=F32)
        acc = p if acc is None else acc + p
    return acc


def _dot_sel_right(x, sel, terms=3):
    acc = None
    for t in _split(x, terms):
        p = jnp.dot(t, sel, preferred_element_type=F32)
        acc = p if acc is None else acc + p
    return acc


def _iota(shape, dim):
    return lax.broadcasted_iota(jnp.int32, shape, dim)


def _seg(shape, dim, seg):
    return lax.shift_right_logical(_iota(shape, dim), int(math.log2(seg)))


def _pos(shape, dim, seg):
    return lax.bitwise_and(_iota(shape, dim), seg - 1)


def _block_ones(n, seg):
    return jnp.where(_seg((n, n), 0, seg) == _seg((n, n), 1, seg), 1.0, 0.0).astype(BF16)


def _block_tri(n, seg, upper):
    r, c = _iota((n, n), 0), _iota((n, n), 1)
    same = _seg((n, n), 0, seg) == _seg((n, n), 1, seg)
    tri = (r <= c) if upper else (r >= c)
    return jnp.where(jnp.logical_and(same, tri), 1.0, 0.0).astype(BF16)


def _segsum(x, ones_blk):
    return _dot_sel_right(x, ones_blk, terms=1)


def _bd_mask(rows, lanes, row_seg, lane_seg):
    return _seg((rows, lanes), 0, row_seg) == _seg((rows, lanes), 1, lane_seg)


def _block_diag(x, mask):
    x16 = x.astype(BF16)
    tall = jnp.concatenate([x16] * (mask.shape[0] // x.shape[0]), axis=0)
    return jnp.where(mask, tall, jnp.zeros_like(tall))


def _sigmoid(x):
    return 0.5 + 0.5 * jnp.tanh(0.5 * x)


def _layer_norm(y, g, b):
    mu = jnp.mean(y, axis=-1, keepdims=True)
    yc = y - mu
    var = jnp.mean(yc * yc, axis=-1, keepdims=True)
    return yc * lax.rsqrt(var + LN_EPS) * g + b


def _mm_kernel(x_ref, w_ref, o_ref):
    o_ref[...] = jnp.dot(x_ref[...], w_ref[...], preferred_element_type=F32).astype(o_ref.dtype)


def _matmul_stack(xs, ws, out_dtype):
    p, m, k = xs.shape
    n = ws.shape[2]
    bm = _pick(m, (2048, 1024, 512, 256, 128))
    bn = _pick(n, (1024, 512, 256, 128))
    return pl.pallas_call(
        _mm_kernel,
        grid=(p, n // bn, m // bm),
        in_specs=[pl.BlockSpec((None, bm, k), lambda q, j, i: (q, i, 0)),
                  pl.BlockSpec((None, k, bn), lambda q, j, i: (q, 0, j))],
        out_specs=pl.BlockSpec((None, bm, bn), lambda q, j, i: (q, i, j)),
        out_shape=jax.ShapeDtypeStruct((p, m, n), out_dtype),
        compiler_params=_params("parallel", "parallel", "parallel"),
        name="matmul_stack",
    )(xs, ws)


def _mm_res_ln_kernel(a_ref, w_ref, res_ref, gb_ref, o32_ref, o16_ref, *, alpha, nk, sub):
    kk = pl.program_id(1)
    blocks = [slice(s * sub, (s + 1) * sub) for s in range(a_ref.shape[0] // sub)]
    part = lambda rows: jnp.dot(a_ref[rows, :], w_ref[...], preferred_element_type=F32)

    @pl.when(kk == 0)
    def _():
        for rows in blocks:
            o32_ref[rows, :] = part(rows)

    @pl.when(kk > 0)
    def _():
        for rows in blocks:
            o32_ref[rows, :] += part(rows)

    @pl.when(kk == nk - 1)
    def _():
        for rows in blocks:
            y = alpha * res_ref[rows, :] + o32_ref[rows, :]
            out = _layer_norm(y, gb_ref[0:1, :], gb_ref[1:2, :])
            o32_ref[rows, :] = out
            o16_ref[rows, :] = out.astype(BF16)


def _matmul_res_ln(a, w, resid, gamma, beta, alpha):
    m, k = a.shape
    d = w.shape[1]
    bm = _pick(m, (1024, 512, 256, 128))
    bk = _pick(k, (512, 256, 128))
    nk = k // bk
    gb = jnp.concatenate([gamma.reshape(1, d), beta.reshape(1, d),
                          jnp.zeros((V7X_SUBLANES - 2, d), F32)], axis=0)
    return pl.pallas_call(
        functools.partial(_mm_res_ln_kernel, alpha=alpha, nk=nk, sub=_pick(bm, (512, 256, 128))),
        grid=(m // bm, nk),
        in_specs=[pl.BlockSpec((bm, bk), lambda i, kk: (i, kk)),
                  pl.BlockSpec((bk, d), lambda i, kk: (kk, 0)),
                  pl.BlockSpec((bm, d), lambda i, kk: (i, 0)),
                  pl.BlockSpec((V7X_SUBLANES, d), lambda i, kk: (0, 0))],
        out_specs=[pl.BlockSpec((bm, d), lambda i, kk: (i, 0)),
                   pl.BlockSpec((bm, d), lambda i, kk: (i, 0))],
        out_shape=[jax.ShapeDtypeStruct((m, d), F32), jax.ShapeDtypeStruct((m, d), BF16)],
        compiler_params=_params("parallel", "arbitrary"),
        name="matmul_res_ln",
    )(a, w, resid, gb)


def _shift_rows(h, prev, shift):
    rolled = pltpu.roll(h, shift=shift, axis=0)
    head = jnp.where(_iota(prev.shape, 0) < shift, pltpu.roll(prev, shift=shift, axis=0), rolled[0:HALO])
    return jnp.concatenate([head, rolled[HALO:]], axis=0)


def _mm_conv_kernel(x_ref, xh_ref, *refs, taps, glu, seq_len, bm):
    if glu:
        wg_ref, wu_ref, cg_ref, cu_ref, o_ref = refs
    else:
        wg_ref, cg_ref, o_ref = refs
    i = pl.program_id(1)
    valid = jnp.where((i * bm) % seq_len == 0, 0.0, 1.0).astype(F32)
    x = x_ref[...]
    xh = xh_ref[...]

    def branch(w_ref, c_ref):
        h = jnp.dot(x, w_ref[...], preferred_element_type=F32)
        hh = jnp.dot(xh, w_ref[...], preferred_element_type=F32) * valid
        c = c_ref[...]
        out = c[taps:taps + 1, :] + c[taps - 1:taps, :] * h
        for k in range(taps - 1):
            out = out + c[k:k + 1, :] * _shift_rows(h, hh, taps - 1 - k)
        return out

    half = 0.5 * branch(wg_ref, cg_ref)
    act = half + half * jnp.tanh(half)
    if glu:
        act = act * branch(wu_ref, cu_ref)
    o_ref[...] = act.astype(o_ref.dtype)


def _conv_table(conv_w, bias, n_pad):
    taps, n = conv_w.shape
    rows = [conv_w, (jnp.zeros((1, n), F32) if bias is None else bias.reshape(1, n)),
            jnp.zeros((V7X_SUBLANES - taps - 1, n), F32)]
    tab = jnp.concatenate(rows, axis=0)
    return jnp.pad(tab, ((0, 0), (0, n_pad - n)))


def _matmul_conv(x16, w, tab, seq_len, taps, out_dtype, w_up=None, tab_up=None):
    m, k = x16.shape
    n = w.shape[1]
    glu = w_up is not None
    bm = _pick(seq_len, ((1024,) if glu else (2048, 1024)) + (512, 256, 128, 64, 32, 16, 8))
    bn = _pick(n, (512, 256, 128))
    hb = bm // HALO
    x_spec = pl.BlockSpec((bm, k), lambda j, i: (i, 0))
    xh_spec = pl.BlockSpec((HALO, k), lambda j, i: (jnp.maximum(i * hb - 1, 0), 0))
    w_spec = pl.BlockSpec((k, bn), lambda j, i: (0, j))
    c_spec = pl.BlockSpec((V7X_SUBLANES, bn), lambda j, i: (0, j))
    if glu:
        in_specs = [x_spec, xh_spec, w_spec, w_spec, c_spec, c_spec]
        args = (x16, x16, w, w_up, tab, tab_up)
    else:
        in_specs = [x_spec, xh_spec, w_spec, c_spec]
        args = (x16, x16, w, tab)
    return pl.pallas_call(
        functools.partial(_mm_conv_kernel, taps=taps, glu=glu, seq_len=seq_len, bm=bm),
        grid=(n // bn, m // bm),
        in_specs=in_specs,
        out_specs=pl.BlockSpec((bm, bn), lambda j, i: (i, j)),
        out_shape=jax.ShapeDtypeStruct((m, n), out_dtype),
        compiler_params=_params("parallel", "parallel"),
        name="matmul_conv_glu" if glu else "matmul_conv",
    )(*args)


def _rwkv_in_kernel(x_ref, xh_ref, mu_ref, wd_ref, xm_ref, lh_ref, *, seq_len, bm, has_vres):
    i = pl.program_id(0)
    valid = jnp.where((i * bm) % seq_len == 0, 0.0, 1.0).astype(F32)
    x = x_ref[...]
    prev_row = xh_ref[HALO - 1:HALO, :] * valid
    rolled = pltpu.roll(x, shift=1, axis=0)
    x_prev = jnp.where(_iota(x.shape, 0) == 0, prev_row, rolled)
    xx = x_prev - x
    mu = mu_ref[...]
    mix = lambda r: (x + xx * mu[r:r + 1, :]).astype(BF16)
    xm_ref[0] = mix(0)
    xm_ref[1] = mix(2)
    xv = mix(3)
    xm_ref[2] = xv
    lp = V7X_LANES
    down = lambda xin, lo, hi: jnp.dot(xin, wd_ref[:, lo:hi], preferred_element_type=F32)
    lh_ref[:, 0:lp] = jnp.tanh(down(mix(1), 0, lp)).astype(BF16)
    lh_ref[:, lp:2 * lp] = down(mix(4), lp, 2 * lp).astype(BF16)
    if has_vres:
        lh_ref[:, 2 * lp:3 * lp] = down(xv, 2 * lp, 3 * lp).astype(BF16)
    else:
        lh_ref[:, 2 * lp:3 * lp] = jnp.zeros((bm, lp), BF16)
    lh_ref[:, 3 * lp:] = jax.nn.sigmoid(down(mix(5), 3 * lp, wd_ref.shape[1])).astype(BF16)


def _rwkv_in(x, mu8, wd, seq_len, has_vres):
    m, d = x.shape
    lh = wd.shape[1]
    bm = _pick(seq_len, (512, 256, 128, 64, 32, 16, 8))
    hb = bm // HALO
    return pl.pallas_call(
        functools.partial(_rwkv_in_kernel, seq_len=seq_len, bm=bm, has_vres=has_vres),
        grid=(m // bm,),
        in_specs=[pl.BlockSpec((bm, d), lambda i: (i, 0)),
                  pl.BlockSpec((HALO, d), lambda i: (jnp.maximum(i * hb - 1, 0), 0)),
                  pl.BlockSpec((V7X_SUBLANES, d), lambda i: (0, 0)),
                  pl.BlockSpec((d, lh), lambda i: (0, 0))],
        out_specs=[pl.BlockSpec((3, bm, d), lambda i: (0, i, 0)),
                   pl.BlockSpec((bm, lh), lambda i: (i, 0))],
        out_shape=[jax.ShapeDtypeStruct((3, m, d), BF16), jax.ShapeDtypeStruct((m, lh), BF16)],
        compiler_params=_params("parallel"),
        name="rwkv_in",
    )(x, x, mu8, wd)


def _lockstep(chains):
    chains = list(chains)
    while chains:
        alive = []
        for ch in chains:
            try:
                next(ch)
                alive.append(ch)
            except StopIteration:
                pass
        chains = alive


def _neumann_chain(nmat, eye_l, bd, out):
    c = nmat.shape[0]
    p = eye_l + nmat
    nk = _dot(nmat, bd(nmat))
    yield
    n = 2
    while 2 * n < c:
        both = _dot(jnp.concatenate([p, nk], axis=0), bd(nk))
        yield
        p = p + both[:c]
        nk = both[c:]
        n *= 2
    out["tm"] = p + _dot(p, bd(nk))
    yield


def _rwkv_chunk_kernel(r_ref, k_ref, v_ref, vf_ref, lh_ref, wup_ref, g2_ref, vec_ref, z_ref,
                       ht_ref, *, has_vres, n_chunks, n_groups):
    c = CHUNK
    hd = RW_HEAD
    lp = V7X_LANES

    @pl.when(pl.program_id(2) == 0)
    def _():
        ht_ref[...] = jnp.zeros(ht_ref.shape, F32)

    ones_blk = _block_ones(GROUP, hd)
    segsum = lambda t: _segsum(t, ones_blk)
    row = _iota((c, GROUP), 0)
    col = _pos((c, GROUP), 1, c)
    strict = row > col
    incl = row >= col
    eye_l = jnp.where(row == col, 1.0, 0.0).astype(F32)
    mask = _bd_mask(GROUP, GROUP, c, hd)
    bd = lambda t: _block_diag(t, mask)
    tri = _block_tri(n_chunks * c, c, upper=False)

    grp = []
    for g in range(n_groups):
        ln = slice(g * GROUP, (g + 1) * GROUP)
        vec = vec_ref[:, ln]
        w0, a0, v0, k_k, k_a, r_k, lnx_g, lnx_b = [vec[i:i + 1, :] for i in range(8)]
        r = r_ref[:, ln]
        k = k_ref[:, ln]
        v = v_ref[:, ln]
        up = lambda lo, hi, w: jnp.dot(lh_ref[:, lo:hi], w, preferred_element_type=F32)
        lw = -math.exp(-0.5) * _sigmoid(w0 + up(0, lp, wup_ref[0, :, ln]))
        a = _sigmoid(a0 + up(lp, 2 * lp, wup_ref[1, :, ln]))
        if has_vres:
            v = v + (vf_ref[:, ln] - v) * _sigmoid(v0 + up(2 * lp, 3 * lp, wup_ref[2, :, ln]))
        gate = up(3 * lp, lh_ref.shape[1], g2_ref[:, ln])
        kk = k * k_k
        kk = kk * lax.rsqrt(segsum(kk * kk) + NORM_EPS)
        k2 = k * (1.0 + (a - 1.0) * k_a)
        bv = kk * a
        cum = _dot_sel_left(tri, lw, terms=2)
        e_inv = jnp.exp(-cum)
        grp.append(dict(r=r, k2=k2, v=v, bv=bv, cum=cum, gate=gate, r_k=r_k, lnx_g=lnx_g, lnx_b=lnx_b,
                        e_inv=e_inv, at=-kk * jnp.exp(cum - lw), rt=r * jnp.exp(cum), bt=bv * e_inv,
                        kt=k2 * e_inv))

    res = {}

    def prepare(g, ci):
        d = grp[g]
        sl = slice(ci * c, (ci + 1) * c)
        vc, at, rt = d["v"][sl], d["at"][sl], d["rt"][sl]
        decay = jnp.exp(d["cum"][(ci + 1) * c - 1:(ci + 1) * c, :])
        e_end = decay * d["e_inv"][sl]
        bk_end = jnp.concatenate([d["bv"][sl] * e_end, d["k2"][sl] * e_end], axis=0)
        lhs = jnp.concatenate([at, rt], axis=0)
        xb = _dot_nt(lhs, bd(d["bt"][sl]))
        xk = _dot_nt(lhs, bd(d["kt"][sl]))
        yield
        aab = jnp.where(strict, xb[:c], 0.0)
        prb = jnp.where(incl, xb[c:], 0.0)
        aak = jnp.where(strict, xk[:c], 0.0)
        prk = jnp.where(incl, xk[c:], 0.0)
        xv = _dot(jnp.concatenate([aak, prk], axis=0), bd(vc))
        inv = {}
        yield from _neumann_chain(aab, eye_l, bd, inv)
        tm = inv["tm"]
        wmat = _dot(tm, bd(at))
        u0 = _dot(tm, bd(xv[:c]))
        yield
        res[g, ci] = dict(lhs=jnp.concatenate([wmat, rt], axis=0), u0=u0, prb=prb, y0=xv[c:], vc=vc,
                          bk_end=bk_end, decay=decay)

    ys = [[] for _ in range(n_groups)]

    def advance(g):
        ht = ht_ref[g]
        for ci in range(n_chunks):
            d = res[g, ci]
            xh = _dot_nt(d["lhs"], ht)
            yield
            u = xh[:c] + d["u0"]
            upd = _dot_tn(jnp.concatenate([u, d["vc"]], axis=0), d["bk_end"])
            ys[g].append(xh[c:] + _dot(d["prb"], bd(u)) + d["y0"])
            yield
            ht = d["decay"] * ht + jnp.where(mask, upd, 0.0)
        ht_ref[g] = ht

    _lockstep(prepare(g, ci) for ci in range(n_chunks) for g in range(n_groups))
    _lockstep(advance(g) for g in range(n_groups))

    inv_n = 1.0 / hd
    for g in range(n_groups):
        d = grp[g]
        y = jnp.concatenate(ys[g], axis=0)
        mean = segsum(y) * inv_n
        yc = y - mean
        var = segsum(yc * yc) * inv_n
        yn = yc * lax.rsqrt(var + RW_LNX_EPS) * d["lnx_g"] + d["lnx_b"]
        bonus = segsum(d["r"] * d["k2"] * d["r_k"]) * d["v"]
        z_ref[:, g * GROUP:(g + 1) * GROUP] = ((yn + bonus) * d["gate"]).astype(z_ref.dtype)


def _rwkv_chunk(rkv, v_first, lh, wup, g2, vecs, batch, seq_len, has_vres):
    _, m, d = rkv.shape
    lhw = lh.shape[1]
    tb = _pick(seq_len, (128, 64))
    nt = seq_len // tb
    n_groups = _pick(d // GROUP, (8, 4, 2, 1))
    lanes = n_groups * GROUP
    row = lambda b, j, t: b * nt + t
    plane = lambda p: pl.BlockSpec((None, tb, lanes), lambda b, j, t: (p, row(b, j, t), j))
    return pl.pallas_call(
        functools.partial(_rwkv_chunk_kernel, has_vres=has_vres, n_chunks=tb // CHUNK, n_groups=n_groups),
        grid=(batch, d // lanes, nt),
        in_specs=[plane(0), plane(1), plane(2), plane(2),
                  pl.BlockSpec((tb, lhw), lambda b, j, t: (row(b, j, t), 0)),
                  pl.BlockSpec((3, V7X_LANES, lanes), lambda b, j, t: (0, 0, j)),
                  pl.BlockSpec((g2.shape[0], lanes), lambda b, j, t: (0, j)),
                  pl.BlockSpec((V7X_SUBLANES, lanes), lambda b, j, t: (0, j))],
        out_specs=pl.BlockSpec((tb, lanes), lambda b, j, t: (row(b, j, t), j)),
        out_shape=jax.ShapeDtypeStruct((m, d), BF16),
        scratch_shapes=[pltpu.VMEM((n_groups, GROUP, GROUP), F32)],
        compiler_params=_params("parallel", "parallel", "arbitrary"),
        name="rwkv_chunk",
    )(rkv, rkv, rkv, v_first, lh, wup, g2, vecs)


def _gdn_gates_kernel(x_ref, w_ref, wt_ref, ad_ref, adt_ref, gb_ref, gt_ref, *, hv, bm):
    c = CHUNK
    x = x_ref[...]
    ba = jnp.dot(x, w_ref[...], preferred_element_type=F32)
    bat = lax.dot_general(wt_ref[...], x, (((1,), (1,)), ((), ())),
                          preferred_element_type=F32)
    ad = ad_ref[...]
    adt = adt_ref[...]
    g = -jnp.exp(ad[0:1, :]) * jax.nn.softplus(ba[:, hv:] + ad[1:2, :])
    gt = -jnp.exp(adt[:, 0:1]) * jax.nn.softplus(bat[hv:, :] + adt[:, 1:2])
    gb_ref[:, 0:hv] = jax.nn.sigmoid(ba[:, :hv])
    gt_ref[0:hv, :] = jax.nn.sigmoid(bat[:hv, :])
    gb_ref[:, hv:] = _dot_sel_left(_block_tri(bm, c, upper=False), g)
    gt_ref[hv:, :] = _dot_sel_right(gt, _block_tri(bm, c, upper=True))


def _gdn_gates(x16, w_ba, a_log, dt_bias):
    m, d = x16.shape
    hv = a_log.shape[0]
    bm = _pick(m, (512, 256, 128))
    ad = jnp.concatenate([a_log.reshape(1, hv), dt_bias.reshape(1, hv),
                          jnp.zeros((V7X_SUBLANES - 2, hv), F32)], axis=0)
    adt = jnp.transpose(ad)
    return pl.pallas_call(
        functools.partial(_gdn_gates_kernel, hv=hv, bm=bm),
        grid=(m // bm,),
        in_specs=[pl.BlockSpec((bm, d), lambda i: (i, 0)),
                  pl.BlockSpec((d, 2 * hv), lambda i: (0, 0)),
                  pl.BlockSpec((2 * hv, d), lambda i: (0, 0)),
                  pl.BlockSpec((V7X_SUBLANES, hv), lambda i: (0, 0)),
                  pl.BlockSpec((hv, V7X_SUBLANES), lambda i: (0, 0))],
        out_specs=[pl.BlockSpec((bm, 2 * hv), lambda i: (i, 0)),
                   pl.BlockSpec((2 * hv, bm), lambda i: (0, i))],
        out_shape=[jax.ShapeDtypeStruct((m, 2 * hv), F32), jax.ShapeDtypeStruct((2 * hv, m), F32)],
        compiler_params=_params("parallel"),
        name="gdn_gates",
    )(x16, w_ba, jnp.transpose(w_ba), ad, adt)


def _gdn_chunk_kernel(q_ref, k_ref, v_ref, z_ref, gb_ref, grow_ref, brow_ref, nw_ref, o_ref,
                      s_ref, *, hv, n_chunks, n_groups):
    c = CHUNK
    hd = GDN_HEAD
    vh = GROUP // CHUNK
    j = pl.program_id(1)

    @pl.when(pl.program_id(2) == 0)
    def _():
        s_ref[...] = jnp.zeros(s_ref.shape, F32)

    ones_blk = _block_ones(2 * hd, hd)
    segsum = lambda t: _segsum(t, ones_blk)
    l2 = lambda t: t * lax.rsqrt(segsum(t * t) + NORM_EPS)
    row = _iota((c, vh * c), 0)
    col = _pos((c, vh * c), 1, c)
    strict = row > col
    incl = row >= col
    eye_l = jnp.where(row == col, 1.0, 0.0).astype(F32)
    kmask = _bd_mask(vh * c, 2 * hd, 2 * c, hd)
    pair_mask = _bd_mask(2 * hd, 2 * hd, hd, hd)
    cc_mask = _bd_mask(vh * c, vh * c, c, c)
    wide_mask = _bd_mask(vh * c, vh * hd, c, hd)
    rep = lambda t: jnp.concatenate([t[:, :hd], t[:, :hd], t[:, hd:], t[:, hd:]], axis=1)
    bd = lambda t: _block_diag(t, wide_mask)
    bd_cc = lambda t: _block_diag(t, cc_mask)
    pairs = [slice(p * 2 * hd, (p + 1) * 2 * hd) for p in range(2)]

    gb = gb_ref[...]
    src = _iota((2 * hv, vh * c), 0)
    src2 = _iota((2 * hv, vh * hd), 0)
    grp = []
    for g in range(n_groups):
        first = vh * (n_groups * j + g)
        sel64 = lambda base: jnp.where(src == base + first + _seg((2 * hv, vh * c), 1, c),
                                       1.0, 0.0).astype(BF16)
        sel128 = jnp.where(src2 == hv + first + _seg((2 * hv, vh * hd), 1, hd), 1.0, 0.0).astype(BF16)
        gcol128 = _dot_sel_right(gb, sel128)
        grp.append(dict(q=l2(q_ref[:, g * 2 * hd:(g + 1) * 2 * hd]) * (hd ** -0.5),
                        k=l2(k_ref[:, g * 2 * hd:(g + 1) * 2 * hd]),
                        v=v_ref[:, g * vh * hd:(g + 1) * vh * hd],
                        bcol=_dot_sel_right(gb, sel64(0), terms=2), gcol=_dot_sel_right(gb, sel64(hv)),
                        gcol128=gcol128, e_g=jnp.exp(gcol128)))

    res = {}

    def prepare(g, ci):
        d = grp[g]
        sl = slice(ci * c, (ci + 1) * c)
        qc, kc, vc = d["q"][sl], d["k"][sl], d["v"][sl]
        gcol128 = d["gcol128"][sl]
        grow = grow_ref[ci:ci + 1, g * vh * c:(g + 1) * vh * c]
        brow = brow_ref[ci:ci + 1, g * vh * c:(g + 1) * vh * c]
        decay = jnp.exp(jnp.where(incl, d["gcol"][sl] - grow, 0.0))
        both = _dot_nt(jnp.concatenate([kc, qc], axis=0), _block_diag(kc, kmask))
        yield
        amat = jnp.where(strict, both[:c] * decay * d["bcol"][sl], 0.0)
        attn = jnp.where(incl, both[c:] * decay, 0.0)
        inv = {}
        yield from _neumann_chain(-amat, eye_l, bd_cc, inv)
        tm = inv["tm"]
        k4 = rep(kc)
        u = _dot(tm * brow, bd(vc))
        wk = _dot(tm * (brow * jnp.exp(grow)), bd(k4))
        yield
        glast = gcol128[c - 1:c, :]
        q4 = rep(qc)
        res[g, ci] = dict(lhs=[jnp.concatenate([wk[:, ln], q4[:, ln]], axis=0) for ln in pairs],
                          u=u, attn=attn, k4=k4, tail=jnp.exp(glast - gcol128), e_last=jnp.exp(glast),
                          e_g=d["e_g"][sl])

    os_ = [[] for _ in range(n_groups)]

    def advance(g):
        s_pair = [s_ref[g, 0], s_ref[g, 1]]
        for ci in range(n_chunks):
            d = res[g, ci]
            ws = [_dot(d["lhs"][p], s_pair[p]) for p in range(2)]
            yield
            v_new = d["u"] - jnp.concatenate([w[:c] for w in ws], axis=1)
            vs = v_new * d["tail"]
            upd = [_dot_tn(d["k4"][:, ln], vs[:, ln]) for ln in pairs]
            os_[g].append(jnp.concatenate([w[c:] for w in ws], axis=1) * d["e_g"] + _dot(d["attn"], bd(v_new)))
            yield
            s_pair = [s_pair[p] * d["e_last"][:, pairs[p]] + jnp.where(pair_mask, upd[p], 0.0)
                      for p in range(2)]
        s_ref[g, 0] = s_pair[0]
        s_ref[g, 1] = s_pair[1]

    _lockstep(prepare(g, ci) for ci in range(n_chunks) for g in range(n_groups))
    _lockstep(advance(g) for g in range(n_groups))

    ones4 = _block_ones(vh * hd, hd)
    for g in range(n_groups):
        ln = slice(g * vh * hd, (g + 1) * vh * hd)
        o = jnp.concatenate(os_[g], axis=0)
        ms = _segsum(o * o, ones4) * (1.0 / hd)
        z = z_ref[:, ln]
        o_ref[:, ln] = (o * lax.rsqrt(ms + NORM_EPS) * nw_ref[0:1, :] * (z * jax.nn.sigmoid(z))).astype(o_ref.dtype)


def _gdn_chunk(qkv, zed, gates, g_rows, b_rows, norm_w4, batch, seq_len, key_dim, value_dim):
    m = qkv.shape[0]
    hv = gates.shape[1] // 2
    hd = GDN_HEAD
    tb = _pick(seq_len, (128, 64))
    nt = seq_len // tb
    n_groups = _pick(value_dim // (4 * hd), (8, 4, 2, 1))
    kw = n_groups * 2 * hd
    vw = n_groups * 4 * hd
    kb = key_dim // kw
    vb = 2 * key_dim // vw
    row = lambda b, j, t: b * nt + t
    rows_per_tb = tb // CHUNK
    g_rows = g_rows.reshape(m // tb, rows_per_tb, hv * CHUNK)
    b_rows = b_rows.reshape(m // tb, rows_per_tb, hv * CHUNK)
    rows_spec = pl.BlockSpec((None, rows_per_tb, n_groups * 4 * CHUNK), lambda b, j, t: (row(b, j, t), 0, j))
    return pl.pallas_call(
        functools.partial(_gdn_chunk_kernel, hv=hv, n_chunks=tb // CHUNK, n_groups=n_groups),
        grid=(batch, value_dim // vw, nt),
        in_specs=[pl.BlockSpec((tb, kw), lambda b, j, t: (row(b, j, t), j)),
                  pl.BlockSpec((tb, kw), lambda b, j, t: (row(b, j, t), kb + j)),
                  pl.BlockSpec((tb, vw), lambda b, j, t: (row(b, j, t), vb + j)),
                  pl.BlockSpec((tb, vw), lambda b, j, t: (row(b, j, t), j)),
                  pl.BlockSpec((tb, 2 * hv), lambda b, j, t: (row(b, j, t), 0)),
                  rows_spec, rows_spec,
                  pl.BlockSpec((V7X_SUBLANES, 4 * hd), lambda b, j, t: (0, 0))],
        out_specs=pl.BlockSpec((tb, vw), lambda b, j, t: (row(b, j, t), j)),
        out_shape=jax.ShapeDtypeStruct((m, value_dim), BF16),
        scratch_shapes=[pltpu.VMEM((n_groups, 2, 2 * hd, 2 * hd), F32)],
        compiler_params=_params("parallel", "parallel", "arbitrary"),
        name="gdn_chunk",
    )(qkv, qkv, qkv, zed, gates, g_rows, b_rows, norm_w4)


def _pad_rows(w, rows):
    return jnp.pad(w, ((0, rows - w.shape[0]), (0, 0)))


def _pad_cols(w, cols):
    return jnp.pad(w, ((0, 0), (0, cols - w.shape[1])))


def _rwkv_layer(x32, v_first, p, batch, seq_len, alpha, ln_g, ln_b):
    m, d = x32.shape
    lp = V7X_LANES
    has_vres = v_first is not None
    mu8 = _pad_rows(p["mu"], V7X_SUBLANES)
    v1 = p["v1"] if has_vres else jnp.zeros((d, lp), F32)
    wd = jnp.concatenate([_pad_cols(p["w1"], lp), _pad_cols(p["a1"], lp), _pad_cols(v1, lp), p["g1"]],
                         axis=1).astype(BF16)
    xm, lh = _rwkv_in(x32, mu8, wd, seq_len, has_vres)
    ws = jnp.stack([p["w_r"], p["w_k"], p["w_v"]]).astype(BF16)
    rkv = _matmul_stack(xm, ws, F32)
    v2 = p["v2"] if has_vres else jnp.zeros((lp, d), F32)
    wup = jnp.stack([_pad_rows(p["w2"], lp), _pad_rows(p["a2"], lp), _pad_rows(v2, lp)]).astype(BF16)
    v0 = p["v0"] if has_vres else jnp.zeros((d,), F32)
    vecs = jnp.stack([p["w0"], p["a0"], v0, p["k_k"], p["k_a"], p["r_k"].reshape(d), p["lnx_g"], p["lnx_b"]])
    vf = v_first if has_vres else rkv
    z = _rwkv_chunk(rkv, vf, lh, wup, p["g2"].astype(BF16), vecs, batch, seq_len, has_vres)
    x32, x16 = _matmul_res_ln(z, p["w_o"].astype(BF16), x32, ln_g, ln_b, alpha)
    return x32, x16, vf


def _gdn_layer(x32, x16, p, batch, seq_len, alpha, ln_g, ln_b):
    m, d = x32.shape
    hv = p["a_log"].shape[0]
    value_dim = p["w_out"].shape[0]
    conv_ch = p["conv_w"].shape[1]
    key_dim = (conv_ch - value_dim) // 2
    w_in = p["w_in"]
    taps = p["conv_w"].shape[0]
    qkv = _matmul_conv(x16, w_in[:, :conv_ch].astype(BF16), _conv_table(p["conv_w"], None, conv_ch),
                       seq_len, taps, F32)
    zed = _matmul_stack(x16[None], w_in[None, :, conv_ch:conv_ch + value_dim].astype(BF16), F32)[0]
    gates, gates_t = _gdn_gates(x16, w_in[:, conv_ch + value_dim:].astype(BF16), p["a_log"], p["dt_bias"])
    rows = lambda t: t.reshape(hv, m // CHUNK, CHUNK).transpose(1, 0, 2).reshape(m // CHUNK, hv * CHUNK)
    b_rows = rows(gates_t[:hv])
    g_rows = rows(gates_t[hv:])
    norm_w4 = jnp.tile(_pad_rows(p["norm_w"].reshape(1, GDN_HEAD), V7X_SUBLANES), (1, 4))
    o = _gdn_chunk(qkv, zed, gates, g_rows, b_rows, norm_w4, batch, seq_len, key_dim, value_dim)
    return _matmul_res_ln(o, p["w_out"].astype(BF16), x32, ln_g, ln_b, alpha)


def _ffn_layer(x32, x16, p, seq_len, alpha, ln_g, ln_b):
    d_ff = p["w_down"].shape[0]
    ffp = _round_up(d_ff, 512 if d_ff >= 2048 else V7X_LANES)
    w_up, conv_w, conv_b = p["w_up"], p["conv_w"], p["conv_b"]
    wg = _pad_cols(w_up[:, :d_ff], ffp).astype(BF16)
    wu = _pad_cols(w_up[:, d_ff:], ffp).astype(BF16)
    tab_g = _conv_table(conv_w[:, :d_ff], conv_b[:d_ff], ffp)
    tab_u = _conv_table(conv_w[:, d_ff:], conv_b[d_ff:], ffp)
    act = _matmul_conv(x16, wg, tab_g, seq_len, conv_w.shape[0], BF16, w_up=wu, tab_up=tab_u)
    w_down = _pad_rows(p["w_down"], ffp).astype(BF16)
    return _matmul_res_ln(act, w_down, x32, ln_g, ln_b, alpha)


def kernel(x, rw_mu, rw_w_r, rw_w_k, rw_w_v, rw_w_o, rw_w0, rw_w1, rw_w2, rw_a0, rw_a1, rw_a2, rw_v0, rw_v1, rw_v2, rw_g1, rw_g2, rw_k_k, rw_k_a, rw_r_k, rw_lnx_g, rw_lnx_b, gdn_w_in, gdn_conv_w, gdn_a_log, gdn_dt_bias, gdn_norm_w, gdn_w_out, ffn_w_up, ffn_conv_w, ffn_conv_b, ffn_w_down, ln_mix_g, ln_mix_b, ln_ffn_g, ln_ffn_b):
    batch, seq_len, d = x.shape
    depth = ln_mix_g.shape[0]
    alpha = (2 * depth) ** 0.25
    x32 = x.reshape(batch * seq_len, d)
    x16 = None
    v_first = None
    for i in range(depth):
        j = i // 2
        if i % 2 == 0:
            p = dict(mu=rw_mu[j], w_r=rw_w_r[j], w_k=rw_w_k[j], w_v=rw_w_v[j], w_o=rw_w_o[j], w0=rw_w0[j],
                     w1=rw_w1[j], w2=rw_w2[j], a0=rw_a0[j], a1=rw_a1[j], a2=rw_a2[j], g1=rw_g1[j], g2=rw_g2[j],
                     k_k=rw_k_k[j], k_a=rw_k_a[j], r_k=rw_r_k[j], lnx_g=rw_lnx_g[j], lnx_b=rw_lnx_b[j])
            if j > 0:
                p.update(v0=rw_v0[j - 1], v1=rw_v1[j - 1], v2=rw_v2[j - 1])
            x32, x16, v_first = _rwkv_layer(x32, v_first, p, batch, seq_len, alpha, ln_mix_g[i], ln_mix_b[i])
        else:
            p = dict(w_in=gdn_w_in[j], conv_w=gdn_conv_w[j], a_log=gdn_a_log[j], dt_bias=gdn_dt_bias[j],
                     norm_w=gdn_norm_w[j], w_out=gdn_w_out[j])
            x32, x16 = _gdn_layer(x32, x16, p, batch, seq_len, alpha, ln_mix_g[i], ln_mix_b[i])
        p = dict(w_up=ffn_w_up[i], conv_w=ffn_conv_w[i], conv_b=ffn_conv_b[i], w_down=ffn_w_down[i])
        x32, x16 = _ffn_layer(x32, x16, p, seq_len, alpha, ln_ffn_g[i], ln_ffn_b[i])
    return x32.reshape(batch, seq_len, d)
```

```python
import functools
import math

import jax
import jax.numpy as jnp
from jax import lax
from jax.experimental import pallas as pl
from jax.experimental.pallas import tpu as pltpu

F32 = jnp.float32
BF16 = jnp.bfloat16

RW_HEAD = 64
GDN_HEAD = 128
LN_EPS = 1e-5
NORM_EPS = 1e-6
RW_LNX_EPS = 64e-5

V7X_LANES = 128
V7X_SUBLANES = 8
V7X_MXU_DIM = 256
V7X_VMEM_BYTES = 64 * 1024 * 1024
VMEM_LIMIT = V7X_VMEM_BYTES * 7 // 8

CHUNK = 64
GROUP = V7X_MXU_DIM
HALO = V7X_SUBLANES


def _params(*sem):
    return pltpu.CompilerParams(dimension_semantics=sem, vmem_limit_bytes=VMEM_LIMIT)


def _pick(n, prefs):
    for p in prefs:
        if p <= n and n % p == 0:
            return p
    return n


def _round_up(n, m):
    return (n + m - 1) // m * m


def _dot(a, b):
    return jnp.dot(a.astype(BF16), b.astype(BF16), preferred_element_type=F32)


def _dot_nt(a, b):
    return lax.dot_general(a.astype(BF16), b.astype(BF16), (((1,), (1,)), ((), ())),
                           preferred_element_type=F32)


def _dot_tn(a, b):
    return lax.dot_general(a.astype(BF16), b.astype(BF16), (((0,), (0,)), ((), ())),
                           preferred_element_type=F32)


def _split(x, terms):
    out = []
    rest = x
    for _ in range(terms):
        t = rest.astype(BF16)
        out.append(t)
        rest = rest - t.astype(F32)
    return out


def _dot_sel_left(sel, x, terms=3):
    acc = None
    for t in _split(x, terms):
        p = jnp.dot(sel, t, preferred_element_type=F32)
        acc = p if acc is None else acc + p
    return acc


def _dot_sel_right(x, sel, terms=3):
    acc = None
    for t in _split(x, terms):
        p = jnp.dot(t, sel, preferred_element_type=F32)
        acc = p if acc is None else acc + p
    return acc


def _iota(shape, dim):
    return lax.broadcasted_iota(jnp.int32, shape, dim)


def _seg(shape, dim, seg):
    return lax.shift_right_logical(_iota(shape, dim), int(math.log2(seg)))


def _pos(shape, dim, seg):
    return lax.bitwise_and(_iota(shape, dim), seg - 1)


def _block_ones(n, seg):
    return jnp.where(_seg((n, n), 0, seg) == _seg((n, n), 1, seg), 1.0, 0.0).astype(BF16)


def _block_tri(n, seg, upper):
    r, c = _iota((n, n), 0), _iota((n, n), 1)
    same = _seg((n, n), 0, seg) == _seg((n, n), 1, seg)
    tri = (r <= c) if upper else (r >= c)
    return jnp.where(jnp.logical_and(same, tri), 1.0, 0.0).astype(BF16)


def _segsum(x, ones_blk):
    return _dot_sel_right(x, ones_blk, terms=1)


def _bd_mask(rows, lanes, row_seg, lane_seg):
    return _seg((rows, lanes), 0, row_seg) == _seg((rows, lanes), 1, lane_seg)


def _block_diag(x, mask):
    x16 = x.astype(BF16)
    tall = jnp.concatenate([x16] * (mask.shape[0] // x.shape[0]), axis=0)
    return jnp.where(mask, tall, jnp.zeros_like(tall))


def _sigmoid(x):
    return 0.5 + 0.5 * jnp.tanh(0.5 * x)


def _layer_norm(y, g, b):
    mu = jnp.mean(y, axis=-1, keepdims=True)
    yc = y - mu
    var = jnp.mean(yc * yc, axis=-1, keepdims=True)
    return yc * lax.rsqrt(var + LN_EPS) * g + b


def _mm_kernel(x_ref, w_ref, o_ref):
    o_ref[...] = jnp.dot(x_ref[...], w_ref[...], preferred_element_type=F32).astype(o_ref.dtype)


def _matmul_stack(xs, ws, out_dtype):
    p, m, k = xs.shape
    n = ws.shape[2]
    bm = _pick(m, (2048, 1024, 512, 256, 128))
    bn = _pick(n, (1024, 512, 256, 128))
    return pl.pallas_call(
        _mm_kernel,
        grid=(p, n // bn, m // bm),
        in_specs=[pl.BlockSpec((None, bm, k), lambda q, j, i: (q, i, 0)),
                  pl.BlockSpec((None, k, bn), lambda q, j, i: (q, 0, j))],
        out_specs=pl.BlockSpec((None, bm, bn), lambda q, j, i: (q, i, j)),
        out_shape=jax.ShapeDtypeStruct((p, m, n), out_dtype),
        compiler_params=_params("parallel", "parallel", "parallel"),
        name="matmul_stack",
    )(xs, ws)


def _mm_res_ln_kernel(a_ref, w_ref, res_ref, gb_ref, o32_ref, o16_ref, *, alpha, nk, sub):
    kk = pl.program_id(1)
    blocks = [slice(s * sub, (s + 1) * sub) for s in range(a_ref.shape[0] // sub)]
    part = lambda rows: jnp.dot(a_ref[rows, :], w_ref[...], preferred_element_type=F32)

    @pl.when(kk == 0)
    def _():
        for rows in blocks:
            o32_ref[rows, :] = part(rows)

    @pl.when(kk > 0)
    def _():
        for rows in blocks:
            o32_ref[rows, :] += part(rows)

    @pl.when(kk == nk - 1)
    def _():
        for rows in blocks:
            y = alpha * res_ref[rows, :] + o32_ref[rows, :]
            out = _layer_norm(y, gb_ref[0:1, :], gb_ref[1:2, :])
            o32_ref[rows, :] = out
            o16_ref[rows, :] = out.astype(BF16)


def _matmul_res_ln(a, w, resid, gamma, beta, alpha):
    m, k = a.shape
    d = w.shape[1]
    bm = _pick(m, (1024, 512, 256, 128))
    bk = _pick(k, (512, 256, 128))
    nk = k // bk
    gb = jnp.concatenate([gamma.reshape(1, d), beta.reshape(1, d),
                          jnp.zeros((V7X_SUBLANES - 2, d), F32)], axis=0)
    return pl.pallas_call(
        functools.partial(_mm_res_ln_kernel, alpha=alpha, nk=nk, sub=_pick(bm, (512, 256, 128))),
        grid=(m // bm, nk),
        in_specs=[pl.BlockSpec((bm, bk), lambda i, kk: (i, kk)),
                  pl.BlockSpec((bk, d), lambda i, kk: (kk, 0)),
                  pl.BlockSpec((bm, d), lambda i, kk: (i, 0)),
                  pl.BlockSpec((V7X_SUBLANES, d), lambda i, kk: (0, 0))],
        out_specs=[pl.BlockSpec((bm, d), lambda i, kk: (i, 0)),
                   pl.BlockSpec((bm, d), lambda i, kk: (i, 0))],
        out_shape=[jax.ShapeDtypeStruct((m, d), F32), jax.ShapeDtypeStruct((m, d), BF16)],
        compiler_params=_params("parallel", "arbitrary"),
        name="matmul_res_ln",
    )(a, w, resid, gb)


def _shift_rows(h, prev, shift):
    rolled = pltpu.roll(h, shift=shift, axis=0)
    head = jnp.where(_iota(prev.shape, 0) < shift, pltpu.roll(prev, shift=shift, axis=0), rolled[0:HALO])
    return jnp.concatenate([head, rolled[HALO:]], axis=0)


def _mm_conv_kernel(x_ref, xh_ref, *refs, taps, glu, seq_len, bm):
    if glu:
        wg_ref, wu_ref, cg_ref, cu_ref, o_ref = refs
    else:
        wg_ref, cg_ref, o_ref = refs
    i = pl.program_id(1)
    valid = jnp.where((i * bm) % seq_len == 0, 0.0, 1.0).astype(F32)
    x = x_ref[...]
    xh = xh_ref[...]

    def branch(w_ref, c_ref):
        h = jnp.dot(x, w_ref[...], preferred_element_type=F32)
        hh = jnp.dot(xh, w_ref[...], preferred_element_type=F32) * valid
        c = c_ref[...]
        out = c[taps:taps + 1, :] + c[taps - 1:taps, :] * h
        for k in range(taps - 1):
            out = out + c[k:k + 1, :] * _shift_rows(h, hh, taps - 1 - k)
        return out

    half = 0.5 * branch(wg_ref, cg_ref)
    act = half + half * jnp.tanh(half)
    if glu:
        act = act * branch(wu_ref, cu_ref)
    o_ref[...] = act.astype(o_ref.dtype)


def _conv_table(conv_w, bias, n_pad):
    taps, n = conv_w.shape
    rows = [conv_w, (jnp.zeros((1, n), F32) if bias is None else bias.reshape(1, n)),
            jnp.zeros((V7X_SUBLANES - taps - 1, n), F32)]
    tab = jnp.concatenate(rows, axis=0)
    return jnp.pad(tab, ((0, 0), (0, n_pad - n)))


def _matmul_conv(x16, w, tab, seq_len, taps, out_dtype, w_up=None, tab_up=None):
    m, k = x16.shape
    n = w.shape[1]
    glu = w_up is not None
    bm = _pick(seq_len, ((1024,) if glu else (2048, 1024)) + (512, 256, 128, 64, 32, 16, 8))
    bn = _pick(n, (512, 256, 128))
    hb = bm // HALO
    x_spec = pl.BlockSpec((bm, k), lambda j, i: (i, 0))
    xh_spec = pl.BlockSpec((HALO, k), lambda j, i: (jnp.maximum(i * hb - 1, 0), 0))
    w_spec = pl.BlockSpec((k, bn), lambda j, i: (0, j))
    c_spec = pl.BlockSpec((V7X_SUBLANES, bn), lambda j, i: (0, j))
    if glu:
        in_specs = [x_spec, xh_spec, w_spec, w_spec, c_spec, c_spec]
        args = (x16, x16, w, w_up, tab, tab_up)
    else:
        in_specs = [x_spec, xh_spec, w_spec, c_spec]
        args = (x16, x16, w, tab)
    return pl.pallas_call(
        functools.partial(_mm_conv_kernel, taps=taps, glu=glu, seq_len=seq_len, bm=bm),
        grid=(n // bn, m // bm),
        in_specs=in_specs,
        out_specs=pl.BlockSpec((bm, bn), lambda j, i: (i, j)),
        out_shape=jax.ShapeDtypeStruct((m, n), out_dtype),
        compiler_params=_params("parallel", "parallel"),
        name="matmul_conv_glu" if glu else "matmul_conv",
    )(*args)


def _rwkv_in_kernel(x_ref, xh_ref, mu_ref, wd_ref, xm_ref, lh_ref, *, seq_len, bm, has_vres):
    i = pl.program_id(0)
    valid = jnp.where((i * bm) % seq_len == 0, 0.0, 1.0).astype(F32)
    x = x_ref[...]
    prev_row = xh_ref[HALO - 1:HALO, :] * valid
    rolled = pltpu.roll(x, shift=1, axis=0)
    x_prev = jnp.where(_iota(x.shape, 0) == 0, prev_row, rolled)
    xx = x_prev - x
    mu = mu_ref[...]
    mix = lambda r: (x + xx * mu[r:r + 1, :]).astype(BF16)
    xm_ref[0] = mix(0)
    xm_ref[1] = mix(2)
    xv = mix(3)
    xm_ref[2] = xv
    lp = V7X_LANES
    down = lambda xin, lo, hi: jnp.dot(xin, wd_ref[:, lo:hi], preferred_element_type=F32)
    lh_ref[:, 0:lp] = jnp.tanh(down(mix(1), 0, lp)).astype(BF16)
    lh_ref[:, lp:2 * lp] = down(mix(4), lp, 2 * lp).astype(BF16)
    if has_vres:
        lh_ref[:, 2 * lp:3 * lp] = down(xv, 2 * lp, 3 * lp).astype(BF16)
    else:
        lh_ref[:, 2 * lp:3 * lp] = jnp.zeros((bm, lp), BF16)
    lh_ref[:, 3 * lp:] = jax.nn.sigmoid(down(mix(5), 3 * lp, wd_ref.shape[1])).astype(BF16)


def _rwkv_in(x, mu8, wd, seq_len, has_vres):
    m, d = x.shape
    lh = wd.shape[1]
    bm = _pick(seq_len, (512, 256, 128, 64, 32, 16, 8))
    hb = bm // HALO
    return pl.pallas_call(
        functools.partial(_rwkv_in_kernel, seq_len=seq_len, bm=bm, has_vres=has_vres),
        grid=(m // bm,),
        in_specs=[pl.BlockSpec((bm, d), lambda i: (i, 0)),
                  pl.BlockSpec((HALO, d), lambda i: (jnp.maximum(i * hb - 1, 0), 0)),
                  pl.BlockSpec((V7X_SUBLANES, d), lambda i: (0, 0)),
                  pl.BlockSpec((d, lh), lambda i: (0, 0))],
        out_specs=[pl.BlockSpec((3, bm, d), lambda i: (0, i, 0)),
                   pl.BlockSpec((bm, lh), lambda i: (i, 0))],
        out_shape=[jax.ShapeDtypeStruct((3, m, d), BF16), jax.ShapeDtypeStruct((m, lh), BF16)],
        compiler_params=_params("parallel"),
        name="rwkv_in",
    )(x, x, mu8, wd)


def _lockstep(chains):
    chains = list(chains)
    while chains:
        alive = []
        for ch in chains:
            try:
                next(ch)
                alive.append(ch)
            except StopIteration:
                pass
        chains = alive


def _neumann_chain(nmat, eye_l, bd, out):
    c = nmat.shape[0]
    p = eye_l + nmat
    nk = _dot(nmat, bd(nmat))
    yield
    n = 2
    while 2 * n < c:
        both = _dot(jnp.concatenate([p, nk], axis=0), bd(nk))
        yield
        p = p + both[:c]
        nk = both[c:]
        n *= 2
    out["tm"] = p + _dot(p, bd(nk))
    yield


def _rwkv_chunk_kernel(r_ref, k_ref, v_ref, vf_ref, lh_ref, wup_ref, g2_ref, vec_ref, z_ref,
                       ht_ref, *, has_vres, n_chunks, n_groups):
    c = CHUNK
    hd = RW_HEAD
    lp = V7X_LANES

    @pl.when(pl.program_id(2) == 0)
    def _():
        ht_ref[...] = jnp.zeros(ht_ref.shape, F32)

    ones_blk = _block_ones(GROUP, hd)
    segsum = lambda t: _segsum(t, ones_blk)
    row = _iota((c, GROUP), 0)
    col = _pos((c, GROUP), 1, c)
    strict = row > col
    incl = row >= col
    eye_l = jnp.where(row == col, 1.0, 0.0).astype(F32)
    mask = _bd_mask(GROUP, GROUP, c, hd)
    bd = lambda t: _block_diag(t, mask)
    tri = _block_tri(n_chunks * c, c, upper=False)

    grp = []
    for g in range(n_groups):
        ln = slice(g * GROUP, (g + 1) * GROUP)
        vec = vec_ref[:, ln]
        w0, a0, v0, k_k, k_a, r_k, lnx_g, lnx_b = [vec[i:i + 1, :] for i in range(8)]
        r = r_ref[:, ln]
        k = k_ref[:, ln]
        v = v_ref[:, ln]
        up = lambda lo, hi, w: jnp.dot(lh_ref[:, lo:hi], w, preferred_element_type=F32)
        lw = -math.exp(-0.5) * _sigmoid(w0 + up(0, lp, wup_ref[0, :, ln]))
        a = _sigmoid(a0 + up(lp, 2 * lp, wup_ref[1, :, ln]))
        if has_vres:
            v = v + (vf_ref[:, ln] - v) * _sigmoid(v0 + up(2 * lp, 3 * lp, wup_ref[2, :, ln]))
        gate = up(3 * lp, lh_ref.shape[1], g2_ref[:, ln])
        kk = k * k_k
        kk = kk * lax.rsqrt(segsum(kk * kk) + NORM_EPS)
        k2 = k * (1.0 + (a - 1.0) * k_a)
        bv = kk * a
        cum = _dot_sel_left(tri, lw, terms=2)
        e_inv = jnp.exp(-cum)
        grp.append(dict(r=r, k2=k2, v=v, bv=bv, cum=cum, gate=gate, r_k=r_k, lnx_g=lnx_g, lnx_b=lnx_b,
                        e_inv=e_inv, at=-kk * jnp.exp(cum - lw), rt=r * jnp.exp(cum), bt=bv * e_inv,
                        kt=k2 * e_inv))

    res = {}

    def prepare(g, ci):
        d = grp[g]
        sl = slice(ci * c, (ci + 1) * c)
        vc, at, rt = d["v"][sl], d["at"][sl], d["rt"][sl]
        decay = jnp.exp(d["cum"][(ci + 1) * c - 1:(ci + 1) * c, :])
        e_end = decay * d["e_inv"][sl]
        bk_end = jnp.concatenate([d["bv"][sl] * e_end, d["k2"][sl] * e_end], axis=0)
        lhs = jnp.concatenate([at, rt], axis=0)
        xb = _dot_nt(lhs, bd(d["bt"][sl]))
        xk = _dot_nt(lhs, bd(d["kt"][sl]))
        yield
        aab = jnp.where(strict, xb[:c], 0.0)
        prb = jnp.where(incl, xb[c:], 0.0)
        aak = jnp.where(strict, xk[:c], 0.0)
        prk = jnp.where(incl, xk[c:], 0.0)
        xv = _dot(jnp.concatenate([aak, prk], axis=0), bd(vc))
        inv = {}
        yield from _neumann_chain(aab, eye_l, bd, inv)
        tm = inv["tm"]
        wmat = _dot(tm, bd(at))
        u0 = _dot(tm, bd(xv[:c]))
        yield
        res[g, ci] = dict(lhs=jnp.concatenate([wmat, rt], axis=0), u0=u0, prb=prb, y0=xv[c:], vc=vc,
                          bk_end=bk_end, decay=decay)

    ys = [[] for _ in range(n_groups)]

    def advance(g):
        ht = ht_ref[g]
        for ci in range(n_chunks):
            d = res[g, ci]
            xh = _dot_nt(d["lhs"], ht)
            yield
            u = xh[:c] + d["u0"]
            upd = _dot_tn(jnp.concatenate([u, d["vc"]], axis=0), d["bk_end"])
            ys[g].append(xh[c:] + _dot(d["prb"], bd(u)) + d["y0"])
            yield
            ht = d["decay"] * ht + jnp.where(mask, upd, 0.0)
        ht_ref[g] = ht

    _lockstep(prepare(g, ci) for ci in range(n_chunks) for g in range(n_groups))
    _lockstep(advance(g) for g in range(n_groups))

    inv_n = 1.0 / hd
    for g in range(n_groups):
        d = grp[g]
        y = jnp.concatenate(ys[g], axis=0)
        mean = segsum(y) * inv_n
        yc = y - mean
        var = segsum(yc * yc) * inv_n
        yn = yc * lax.rsqrt(var + RW_LNX_EPS) * d["lnx_g"] + d["lnx_b"]
        bonus = segsum(d["r"] * d["k2"] * d["r_k"]) * d["v"]
        z_ref[:, g * GROUP:(g + 1) * GROUP] = ((yn + bonus) * d["gate"]).astype(z_ref.dtype)


def _rwkv_chunk(rkv, v_first, lh, wup, g2, vecs, batch, seq_len, has_vres):
    _, m, d = rkv.shape
    lhw = lh.shape[1]
    tb = _pick(seq_len, (256, 128, 64))
    nt = seq_len // tb
    n_groups = _pick(d // GROUP, (4, 2, 1))
    lanes = n_groups * GROUP
    row = lambda b, j, t: b * nt + t
    plane = lambda p: pl.BlockSpec((None, tb, lanes), lambda b, j, t: (p, row(b, j, t), j))
    return pl.pallas_call(
        functools.partial(_rwkv_chunk_kernel, has_vres=has_vres, n_chunks=tb // CHUNK, n_groups=n_groups),
        grid=(batch, d // lanes, nt),
        in_specs=[plane(0), plane(1), plane(2), plane(2),
                  pl.BlockSpec((tb, lhw), lambda b, j, t: (row(b, j, t), 0)),
                  pl.BlockSpec((3, V7X_LANES, lanes), lambda b, j, t: (0, 0, j)),
                  pl.BlockSpec((g2.shape[0], lanes), lambda b, j, t: (0, j)),
                  pl.BlockSpec((V7X_SUBLANES, lanes), lambda b, j, t: (0, j))],
        out_specs=pl.BlockSpec((tb, lanes), lambda b, j, t: (row(b, j, t), j)),
        out_shape=jax.ShapeDtypeStruct((m, d), BF16),
        scratch_shapes=[pltpu.VMEM((n_groups, GROUP, GROUP), F32)],
        compiler_params=_params("parallel", "parallel", "arbitrary"),
        name="rwkv_chunk",
    )(rkv, rkv, rkv, v_first, lh, wup, g2, vecs)


def _gdn_gates_kernel(x_ref, w_ref, wt_ref, ad_ref, adt_ref, gb_ref, gt_ref, *, hv, bm):
    c = CHUNK
    x = x_ref[...]
    ba = jnp.dot(x, w_ref[...], preferred_element_type=F32)
    bat = lax.dot_general(wt_ref[...], x, (((1,), (1,)), ((), ())),
                          preferred_element_type=F32)
    ad = ad_ref[...]
    adt = adt_ref[...]
    g = -jnp.exp(ad[0:1, :]) * jax.nn.softplus(ba[:, hv:] + ad[1:2, :])
    gt = -jnp.exp(adt[:, 0:1]) * jax.nn.softplus(bat[hv:, :] + adt[:, 1:2])
    gb_ref[:, 0:hv] = jax.nn.sigmoid(ba[:, :hv])
    gt_ref[0:hv, :] = jax.nn.sigmoid(bat[:hv, :])
    gb_ref[:, hv:] = _dot_sel_left(_block_tri(bm, c, upper=False), g)
    gt_ref[hv:, :] = _dot_sel_right(gt, _block_tri(bm, c, upper=True))


def _gdn_gates(x16, w_ba, a_log, dt_bias):
    m, d = x16.shape
    hv = a_log.shape[0]
    bm = _pick(m, (512, 256, 128))
    ad = jnp.concatenate([a_log.reshape(1, hv), dt_bias.reshape(1, hv),
                          jnp.zeros((V7X_SUBLANES - 2, hv), F32)], axis=0)
    adt = jnp.transpose(ad)
    return pl.pallas_call(
        functools.partial(_gdn_gates_kernel, hv=hv, bm=bm),
        grid=(m // bm,),
        in_specs=[pl.BlockSpec((bm, d), lambda i: (i, 0)),
                  pl.BlockSpec((d, 2 * hv), lambda i: (0, 0)),
                  pl.BlockSpec((2 * hv, d), lambda i: (0, 0)),
                  pl.BlockSpec((V7X_SUBLANES, hv), lambda i: (0, 0)),
                  pl.BlockSpec((hv, V7X_SUBLANES), lambda i: (0, 0))],
        out_specs=[pl.BlockSpec((bm, 2 * hv), lambda i: (i, 0)),
                   pl.BlockSpec((2 * hv, bm), lambda i: (0, i))],
        out_shape=[jax.ShapeDtypeStruct((m, 2 * hv), F32), jax.ShapeDtypeStruct((2 * hv, m), F32)],
        compiler_params=_params("parallel"),
        name="gdn_gates",
    )(x16, w_ba, jnp.transpose(w_ba), ad, adt)


def _gdn_chunk_kernel(q_ref, k_ref, v_ref, z_ref, gb_ref, grow_ref, brow_ref, nw_ref, o_ref,
                      s_ref, *, hv, n_chunks, n_groups):
    c = CHUNK
    hd = GDN_HEAD
    vh = GROUP // CHUNK
    j = pl.program_id(1)

    @pl.when(pl.program_id(2) == 0)
    def _():
        s_ref[...] = jnp.zeros(s_ref.shape, F32)

    ones_blk = _block_ones(2 * hd, hd)
    segsum = lambda t: _segsum(t, ones_blk)
    l2 = lambda t: t * lax.rsqrt(segsum(t * t) + NORM_EPS)
    row = _iota((c, vh * c), 0)
    col = _pos((c, vh * c), 1, c)
    strict = row > col
    incl = row >= col
    eye_l = jnp.where(row == col, 1.0, 0.0).astype(F32)
    kmask = _bd_mask(vh * c, 2 * hd, 2 * c, hd)
    pair_mask = _bd_mask(2 * hd, 2 * hd, hd, hd)
    cc_mask = _bd_mask(vh * c, vh * c, c, c)
    wide_mask = _bd_mask(vh * c, vh * hd, c, hd)
    rep = lambda t: jnp.concatenate([t[:, :hd], t[:, :hd], t[:, hd:], t[:, hd:]], axis=1)
    bd = lambda t: _block_diag(t, wide_mask)
    bd_cc = lambda t: _block_diag(t, cc_mask)
    pairs = [slice(p * 2 * hd, (p + 1) * 2 * hd) for p in range(2)]

    gb = gb_ref[...]
    src = _iota((2 * hv, vh * c), 0)
    src2 = _iota((2 * hv, vh * hd), 0)
    grp = []
    for g in range(n_groups):
        first = vh * (n_groups * j + g)
        sel64 = lambda base: jnp.where(src == base + first + _seg((2 * hv, vh * c), 1, c),
                                       1.0, 0.0).astype(BF16)
        sel128 = jnp.where(src2 == hv + first + _seg((2 * hv, vh * hd), 1, hd), 1.0, 0.0).astype(BF16)
        gcol128 = _dot_sel_right(gb, sel128)
        grp.append(dict(q=l2(q_ref[:, g * 2 * hd:(g + 1) * 2 * hd]) * (hd ** -0.5),
                        k=l2(k_ref[:, g * 2 * hd:(g + 1) * 2 * hd]),
                        v=v_ref[:, g * vh * hd:(g + 1) * vh * hd],
                        bcol=_dot_sel_right(gb, sel64(0), terms=2), gcol=_dot_sel_right(gb, sel64(hv)),
                        gcol128=gcol128, e_g=jnp.exp(gcol128)))

    res = {}

    def prepare(g, ci):
        d = grp[g]
        sl = slice(ci * c, (ci + 1) * c)
        qc, kc, vc = d["q"][sl], d["k"][sl], d["v"][sl]
        gcol128 = d["gcol128"][sl]
        grow = grow_ref[ci:ci + 1, g * vh * c:(g + 1) * vh * c]
        brow = brow_ref[ci:ci + 1, g * vh * c:(g + 1) * vh * c]
        decay = jnp.exp(jnp.where(incl, d["gcol"][sl] - grow, 0.0))
        both = _dot_nt(jnp.concatenate([kc, qc], axis=0), _block_diag(kc, kmask))
        yield
        amat = jnp.where(strict, both[:c] * decay * d["bcol"][sl], 0.0)
        attn = jnp.where(incl, both[c:] * decay, 0.0)
        inv = {}
        yield from _neumann_chain(-amat, eye_l, bd_cc, inv)
        tm = inv["tm"]
        k4 = rep(kc)
        u = _dot(tm * brow, bd(vc))
        wk = _dot(tm * (brow * jnp.exp(grow)), bd(k4))
        yield
        glast = gcol128[c - 1:c, :]
        q4 = rep(qc)
        res[g, ci] = dict(lhs=[jnp.concatenate([wk[:, ln], q4[:, ln]], axis=0) for ln in pairs],
                          u=u, attn=attn, k4=k4, tail=jnp.exp(glast - gcol128), e_last=jnp.exp(glast),
                          e_g=d["e_g"][sl])

    os_ = [[] for _ in range(n_groups)]

    def advance(g):
        s_pair = [s_ref[g, 0], s_ref[g, 1]]
        for ci in range(n_chunks):
            d = res[g, ci]
            ws = [_dot(d["lhs"][p], s_pair[p]) for p in range(2)]
            yield
            v_new = d["u"] - jnp.concatenate([w[:c] for w in ws], axis=1)
            vs = v_new * d["tail"]
            upd = [_dot_tn(d["k4"][:, ln], vs[:, ln]) for ln in pairs]
            os_[g].append(jnp.concatenate([w[c:] for w in ws], axis=1) * d["e_g"] + _dot(d["attn"], bd(v_new)))
            yield
            s_pair = [s_pair[p] * d["e_last"][:, pairs[p]] + jnp.where(pair_mask, upd[p], 0.0)
                      for p in range(2)]
        s_ref[g, 0] = s_pair[0]
        s_ref[g, 1] = s_pair[1]

    _lockstep(prepare(g, ci) for ci in range(n_chunks) for g in range(n_groups))
    _lockstep(advance(g) for g in range(n_groups))

    ones4 = _block_ones(vh * hd, hd)
    for g in range(n_groups):
        ln = slice(g * vh * hd, (g + 1) * vh * hd)
        o = jnp.concatenate(os_[g], axis=0)
        ms = _segsum(o * o, ones4) * (1.0 / hd)
        z = z_ref[:, ln]
        o_ref[:, ln] = (o * lax.rsqrt(ms + NORM_EPS) * nw_ref[0:1, :] * (z * jax.nn.sigmoid(z))).astype(o_ref.dtype)


def _gdn_chunk(qkv, zed, gates, g_rows, b_rows, norm_w4, batch, seq_len, key_dim, value_dim):
    m = qkv.shape[0]
    hv = gates.shape[1] // 2
    hd = GDN_HEAD
    tb = _pick(seq_len, (256, 128, 64))
    nt = seq_len // tb
    n_groups = _pick(value_dim // (4 * hd), (4, 2, 1))
    kw = n_groups * 2 * hd
    vw = n_groups * 4 * hd
    kb = key_dim // kw
    vb = 2 * key_dim // vw
    row = lambda b, j, t: b * nt + t
    rows_per_tb = tb // CHUNK
    g_rows = g_rows.reshape(m // tb, rows_per_tb, hv * CHUNK)
    b_rows = b_rows.reshape(m // tb, rows_per_tb, hv * CHUNK)
    rows_spec = pl.BlockSpec((None, rows_per_tb, n_groups * 4 * CHUNK), lambda b, j, t: (row(b, j, t), 0, j))
    return pl.pallas_call(
        functools.partial(_gdn_chunk_kernel, hv=hv, n_chunks=tb // CHUNK, n_groups=n_groups),
        grid=(batch, value_dim // vw, nt),
        in_specs=[pl.BlockSpec((tb, kw), lambda b, j, t: (row(b, j, t), j)),
                  pl.BlockSpec((tb, kw), lambda b, j, t: (row(b, j, t), kb + j)),
                  pl.BlockSpec((tb, vw), lambda b, j, t: (row(b, j, t), vb + j)),
                  pl.BlockSpec((tb, vw), lambda b, j, t: (row(b, j, t), j)),
                  pl.BlockSpec((tb, 2 * hv), lambda b, j, t: (row(b, j, t), 0)),
                  rows_spec, rows_spec,
                  pl.BlockSpec((V7X_SUBLANES, 4 * hd), lambda b, j, t: (0, 0))],
        out_specs=pl.BlockSpec((tb, vw), lambda b, j, t: (row(b, j, t), j)),
        out_shape=jax.ShapeDtypeStruct((m, value_dim), BF16),
        scratch_shapes=[pltpu.VMEM((n_groups, 2, 2 * hd, 2 * hd), F32)],
        compiler_params=_params("parallel", "parallel", "arbitrary"),
        name="gdn_chunk",
    )(qkv, qkv, qkv, zed, gates, g_rows, b_rows, norm_w4)


def _pad_rows(w, rows):
    return jnp.pad(w, ((0, rows - w.shape[0]), (0, 0)))


def _pad_cols(w, cols):
    return jnp.pad(w, ((0, 0), (0, cols - w.shape[1])))


def _rwkv_layer(x32, v_first, p, batch, seq_len, alpha, ln_g, ln_b):
    m, d = x32.shape
    lp = V7X_LANES
    has_vres = v_first is not None
    mu8 = _pad_rows(p["mu"], V7X_SUBLANES)
    v1 = p["v1"] if has_vres else jnp.zeros((d, lp), F32)
    wd = jnp.concatenate([_pad_cols(p["w1"], lp), _pad_cols(p["a1"], lp), _pad_cols(v1, lp), p["g1"]],
                         axis=1).astype(BF16)
    xm, lh = _rwkv_in(x32, mu8, wd, seq_len, has_vres)
    ws = jnp.stack([p["w_r"], p["w_k"], p["w_v"]]).astype(BF16)
    rkv = _matmul_stack(xm, ws, F32)
    v2 = p["v2"] if has_vres else jnp.zeros((lp, d), F32)
    wup = jnp.stack([_pad_rows(p["w2"], lp), _pad_rows(p["a2"], lp), _pad_rows(v2, lp)]).astype(BF16)
    v0 = p["v0"] if has_vres else jnp.zeros((d,), F32)
    vecs = jnp.stack([p["w0"], p["a0"], v0, p["k_k"], p["k_a"], p["r_k"].reshape(d), p["lnx_g"], p["lnx_b"]])
    vf = v_first if has_vres else rkv
    z = _rwkv_chunk(rkv, vf, lh, wup, p["g2"].astype(BF16), vecs, batch, seq_len, has_vres)
    x32, x16 = _matmul_res_ln(z, p["w_o"].astype(BF16), x32, ln_g, ln_b, alpha)
    return x32, x16, vf


def _gdn_layer(x32, x16, p, batch, seq_len, alpha, ln_g, ln_b):
    m, d = x32.shape
    hv = p["a_log"].shape[0]
    value_dim = p["w_out"].shape[0]
    conv_ch = p["conv_w"].shape[1]
    key_dim = (conv_ch - value_dim) // 2
    w_in = p["w_in"]
    taps = p["conv_w"].shape[0]
    qkv = _matmul_conv(x16, w_in[:, :conv_ch].astype(BF16), _conv_table(p["conv_w"], None, conv_ch),
                       seq_len, taps, F32)
    zed = _matmul_stack(x16[None], w_in[None, :, conv_ch:conv_ch + value_dim].astype(BF16), F32)[0]
    gates, gates_t = _gdn_gates(x16, w_in[:, conv_ch + value_dim:].astype(BF16), p["a_log"], p["dt_bias"])
    rows = lambda t: t.reshape(hv, m // CHUNK, CHUNK).transpose(1, 0, 2).reshape(m // CHUNK, hv * CHUNK)
    b_rows = rows(gates_t[:hv])
    g_rows = rows(gates_t[hv:])
    norm_w4 = jnp.tile(_pad_rows(p["norm_w"].reshape(1, GDN_HEAD), V7X_SUBLANES), (1, 4))
    o = _gdn_chunk(qkv, zed, gates, g_rows, b_rows, norm_w4, batch, seq_len, key_dim, value_dim)
    return _matmul_res_ln(o, p["w_out"].astype(BF16), x32, ln_g, ln_b, alpha)


def _ffn_layer(x32, x16, p, seq_len, alpha, ln_g, ln_b):
    d_ff = p["w_down"].shape[0]
    ffp = _round_up(d_ff, 512 if d_ff >= 2048 else V7X_LANES)
    w_up, conv_w, conv_b = p["w_up"], p["conv_w"], p["conv_b"]
    wg = _pad_cols(w_up[:, :d_ff], ffp).astype(BF16)
    wu = _pad_cols(w_up[:, d_ff:], ffp).astype(BF16)
    tab_g = _conv_table(conv_w[:, :d_ff], conv_b[:d_ff], ffp)
    tab_u = _conv_table(conv_w[:, d_ff:], conv_b[d_ff:], ffp)
    act = _matmul_conv(x16, wg, tab_g, seq_len, conv_w.shape[0], BF16, w_up=wu, tab_up=tab_u)
    w_down = _pad_rows(p["w_down"], ffp).astype(BF16)
    return _matmul_res_ln(act, w_down, x32, ln_g, ln_b, alpha)


def kernel(x, rw_mu, rw_w_r, rw_w_k, rw_w_v, rw_w_o, rw_w0, rw_w1, rw_w2, rw_a0, rw_a1, rw_a2, rw_v0, rw_v1, rw_v2, rw_g1, rw_g2, rw_k_k, rw_k_a, rw_r_k, rw_lnx_g, rw_lnx_b, gdn_w_in, gdn_conv_w, gdn_a_log, gdn_dt_bias, gdn_norm_w, gdn_w_out, ffn_w_up, ffn_conv_w, ffn_conv_b, ffn_w_down, ln_mix_g, ln_mix_b, ln_ffn_g, ln_ffn_b):
    batch, seq_len, d = x.shape
    depth = ln_mix_g.shape[0]
    alpha = (2 * depth) ** 0.25
    x32 = x.reshape(batch * seq_len, d)
    x16 = None
    v_first = None
    for i in range(depth):
        j = i // 2
        if i % 2 == 0:
            p = dict(mu=rw_mu[j], w_r=rw_w_r[j], w_k=rw_w_k[j], w_v=rw_w_v[j], w_o=rw_w_o[j], w0=rw_w0[j],
                     w1=rw_w1[j], w2=rw_w2[j], a0=rw_a0[j], a1=rw_a1[j], a2=rw_a2[j], g1=rw_g1[j], g2=rw_g2[j],
                     k_k=rw_k_k[j], k_a=rw_k_a[j], r_k=rw_r_k[j], lnx_g=rw_lnx_g[j], lnx_b=rw_lnx_b[j])
            if j > 0:
                p.update(v0=rw_v0[j - 1], v1=rw_v1[j - 1], v2=rw_v2[j - 1])
            x32, x16, v_first = _rwkv_layer(x32, v_first, p, batch, seq_len, alpha, ln_mix_g[i], ln_mix_b[i])
        else:
            p = dict(w_in=gdn_w_in[j], conv_w=gdn_conv_w[j], a_log=gdn_a_log[j], dt_bias=gdn_dt_bias[j],
                     norm_w=gdn_norm_w[j], w_out=gdn_w_out[j])
            x32, x16 = _gdn_layer(x32, x16, p, batch, seq_len, alpha, ln_mix_g[i], ln_mix_b[i])
        p = dict(w_up=ffn_w_up[i], conv_w=ffn_conv_w[i], conv_b=ffn_conv_b[i], w_down=ffn_w_down[i])
        x32, x16 = _ffn_layer(x32, x16, p, seq_len, alpha, ln_ffn_g[i], ln_ffn_b[i])
    return x32.reshape(batch, seq_len, d)
```
